```python
import jax, jax.numpy as jnp
from jax import lax
import numpy as np

D_MODEL = 1024
BATCH = 8
SEQ = 4096
DEPTH = 2

A_HEADS = 8
A_HEAD_DIM = 64
A_WIDTH = A_HEADS * A_HEAD_DIM
A_DECAY_LORA = 64
A_ICLR_LORA = 64
A_GATE_LORA = 128
A_GN_EPS = 64e-5
A_SIZES = (A_WIDTH, A_WIDTH, A_WIDTH, A_DECAY_LORA, A_ICLR_LORA, A_GATE_LORA)
A_COLS = sum(A_SIZES)

B_HEADS = 4
B_KEY_DIM = 64
B_VAL_DIM = 128
B_KEY_WIDTH = B_HEADS * B_KEY_DIM
B_VAL_WIDTH = B_HEADS * B_VAL_DIM
B_GATE_LORA = 16
B_GATE_TAU = 16.0
B_CHUNK = 64
B_NORM_EPS = 1e-5
B_SIZES = (B_KEY_WIDTH, B_KEY_WIDTH, B_VAL_WIDTH, B_VAL_WIDTH, B_GATE_LORA)
B_COLS = sum(B_SIZES)

EVEN_COLS = A_COLS + B_COLS
EVEN_OUT = A_WIDTH + B_VAL_WIDTH

C_HEADS = 8
C_HEAD_DIM = 128
C_WIDTH = C_HEADS * C_HEAD_DIM
C_IDX_HEADS = 4
C_IDX_DIM = 64
C_INDEX_TOPK = 256
C_QBLOCK = 128
C_SIZES = (C_WIDTH, C_HEAD_DIM, C_HEAD_DIM, C_IDX_HEADS * C_IDX_DIM, C_IDX_DIM, C_IDX_HEADS)
ODD_COLS = sum(C_SIZES)
ROPE_THETA = 10000.0

N_EXPERTS = 16
N_GROUPS = 4
EXPERTS_PER_GROUP = N_EXPERTS // N_GROUPS
TOP_K = 2
D_EXPERT = 256

DN_ALPHA = (2 * DEPTH) ** 0.25
DN_BETA = (8 * DEPTH) ** -0.25
LN_EPS = 1e-5
N_EVEN = (DEPTH + 1) // 2
N_ODD = DEPTH // 2

kernel_name = 'hybrid_rwkv7_gla_dsa_moe_deepnorm'


def _split_cols(p, sizes):
    return jnp.split(p, np.cumsum(sizes)[:-1].tolist(), axis=-1)


def _layer_norm(x, g, b, eps=LN_EPS):
    xf = x.astype(jnp.float32)
    mu = jnp.mean(xf, -1, keepdims=True)
    var = jnp.mean(jnp.square(xf - mu), -1, keepdims=True)
    return ((xf - mu) * lax.rsqrt(var + eps) * g + b).astype(x.dtype)


def _token_shift(z):
    return jnp.pad(z, ((0, 0), (1, 0), (0, 0)))[:, :-1]


def _rope(z, pos):
    half = z.shape[-1] // 2
    inv = ROPE_THETA ** (-jnp.arange(half, dtype=jnp.float32) / half)
    ang = pos[:, None] * inv[None, :]
    cos = jnp.cos(ang)[:, None, :].astype(z.dtype)
    sin = jnp.sin(ang)[:, None, :].astype(z.dtype)
    z1, z2 = z[..., :half], z[..., half:]
    return jnp.concatenate([z1 * cos - z2 * sin, z2 * cos + z1 * sin], axis=-1)


def _rwkv7_group(p, mu, w0, w2, a0, a2, g2, kk_scale, ka_scale, r_k, gn_g, gn_b):
    bsz, t, _ = p.shape
    f32 = jnp.float32
    pm = p + (_token_shift(p) - p) * mu
    r, k, v, xw, xa, xg = _split_cols(pm, A_SIZES)
    w_log = -jax.nn.softplus(-(w0 + jnp.tanh(xw) @ w2).astype(f32)) - 0.5
    decay = jnp.exp(-jnp.exp(w_log))
    a = jax.nn.sigmoid((a0 + xa @ a2).astype(f32))
    g = (jax.nn.sigmoid(xg) @ g2).astype(f32)
    heads = lambda z: z.astype(f32).reshape(bsz, t, A_HEADS, A_HEAD_DIM)
    kk = heads(k * kk_scale)
    kk = kk / jnp.maximum(jnp.linalg.norm(kk, axis=-1, keepdims=True), 1e-12)
    kh = heads(k.astype(f32) * (1.0 + (a - 1.0) * ka_scale))
    rh, vh, wh, ah = heads(r), heads(v), heads(decay), heads(a)

    def step(state, inp):
        r_t, w_t, k_t, v_t, kk_t, a_t = inp
        sa = jnp.einsum('bhvk,bhk->bhv', state, kk_t)
        state = (state * w_t[:, :, None, :] - sa[..., None] * (kk_t * a_t)[:, :, None, :]
                 + v_t[..., None] * k_t[:, :, None, :])
        return state, jnp.einsum('bhvk,bhk->bhv', state, r_t)

    xs = tuple(jnp.swapaxes(z, 0, 1) for z in (rh, wh, kh, vh, kk, ah))
    s0 = jnp.zeros((bsz, A_HEADS, A_HEAD_DIM, A_HEAD_DIM), f32)
    _, y = lax.scan(step, s0, xs)
    y = jnp.swapaxes(y, 0, 1)
    ym = jnp.mean(y, -1, keepdims=True)
    yv = jnp.mean(jnp.square(y - ym), -1, keepdims=True)
    yn = ((y - ym) * lax.rsqrt(yv + A_GN_EPS)).reshape(bsz, t, A_WIDTH) * gn_g + gn_b
    bonus = (jnp.sum(rh * kh * r_k, -1, keepdims=True) * vh).reshape(bsz, t, A_WIDTH)
    return ((yn + bonus) * g).astype(p.dtype)


def _gla_group(p, gate_w2, gate_b, norm_g):
    bsz, t, _ = p.shape
    f32 = jnp.float32
    nc = t // B_CHUNK
    q, k, v, g, xa = _split_cols(p, B_SIZES)
    log_a = jax.nn.log_sigmoid((xa @ gate_w2 + gate_b).astype(f32)) / B_GATE_TAU

    def chunks(z, d):
        return z.astype(f32).reshape(bsz, nc, B_CHUNK, B_HEADS, d).transpose(0, 3, 1, 2, 4)

    qc = chunks(q, B_KEY_DIM) * B_KEY_DIM ** -0.5
    kc = chunks(k, B_KEY_DIM)
    vc = chunks(v, B_VAL_DIM)
    bc = jnp.cumsum(chunks(log_a, B_KEY_DIM), axis=3)
    b_last = bc[:, :, :, -1:, :]
    q_dec = qc * jnp.exp(bc)
    k_inv = kc * jnp.exp(-bc)
    k_end = kc * jnp.exp(b_last - bc)
    causal = jnp.tril(jnp.ones((B_CHUNK, B_CHUNK), bool))
    att = jnp.where(causal, jnp.einsum('bhncd,bhnsd->bhncs', q_dec, k_inv), 0.0)
    o_intra = jnp.einsum('bhncs,bhnse->bhnce', att, vc)
    kv = jnp.einsum('bhncd,bhnce->bhnde', k_end, vc)

    def step(state, inp):
        d_n, kv_n = inp
        return d_n[..., None] * state + kv_n, state

    s0 = jnp.zeros((bsz, B_HEADS, B_KEY_DIM, B_VAL_DIM), f32)
    _, s_prev = lax.scan(step, s0, (jnp.moveaxis(jnp.exp(b_last[:, :, :, 0, :]), 2, 0),
                                    jnp.moveaxis(kv, 2, 0)))
    o_inter = jnp.einsum('bhncd,nbhde->bhnce', q_dec, s_prev)
    o = (o_intra + o_inter).transpose(0, 2, 3, 1, 4).reshape(bsz, t, B_HEADS, B_VAL_DIM)
    o = o * lax.rsqrt(jnp.mean(jnp.square(o), -1, keepdims=True) + B_NORM_EPS)
    o = o.reshape(bsz, t, B_VAL_WIDTH) * norm_g * jax.nn.silu(g.astype(f32))
    return o.astype(p.dtype)


def _dsa_mixer(p, ik_g, ik_b, k_top):
    bsz, t, _ = p.shape
    f32 = jnp.float32
    q, k, v, iq, ik, iw = _split_cols(p, C_SIZES)
    pos = jnp.arange(t, dtype=f32)
    q = _rope(q.reshape(bsz, t, C_HEADS, C_HEAD_DIM), pos)
    k = _rope(k.reshape(bsz, t, 1, C_HEAD_DIM), pos)[:, :, 0]
    iq = _rope(iq.reshape(bsz, t, C_IDX_HEADS, C_IDX_DIM), pos)
    ik = _rope(_layer_norm(ik, ik_g, ik_b).reshape(bsz, t, 1, C_IDX_DIM), pos)[:, :, 0]
    iw = iw.astype(f32) * C_IDX_HEADS ** -0.5
    nb = t // C_QBLOCK
    key_pos = jnp.arange(t)
    to_blocks = lambda z: jnp.swapaxes(z.reshape(bsz, nb, C_QBLOCK, *z.shape[2:]), 0, 1)
    gather_rows = jax.vmap(lambda src, idx: src[idx])

    def block(args):
        qb, iqb, iwb, qpos = args
        s = jnp.einsum('bqhd,bsd->bhqs', iqb, ik).astype(f32) * C_IDX_DIM ** -0.5
        score = jnp.einsum('bhqs,bqh->bqs', jax.nn.relu(s), iwb)
        score = jnp.where(key_pos[None, None, :] <= qpos[None, :, None], score, -jnp.inf)
        _, idx = lax.top_k(score, k_top)
        valid = idx <= qpos[None, :, None]
        k_sel = gather_rows(k, idx)
        v_sel = gather_rows(v, idx)
        logits = jnp.einsum('bqhd,bqkd->bqhk', qb, k_sel).astype(f32) * C_HEAD_DIM ** -0.5
        logits = jnp.where(valid[:, :, None, :], logits, -jnp.inf)
        prob = jax.nn.softmax(logits, axis=-1).astype(v_sel.dtype)
        return jnp.einsum('bqhk,bqkd->bqhd', prob, v_sel)

    out = lax.map(block, (to_blocks(q), to_blocks(iq), to_blocks(iw), key_pos.reshape(nb, C_QBLOCK)))
    return jnp.swapaxes(out, 0, 1).reshape(bsz, t, C_WIDTH)


def _moe(h, router_w, router_bias, w_gate, w_up, w_down):
    bsz, t, d = h.shape
    f32 = jnp.float32
    hf = h.reshape(-1, d)
    s = jax.nn.sigmoid((hf @ router_w).astype(f32))
    sel = s + router_bias
    group_score = lax.top_k(sel.reshape(-1, N_GROUPS, EXPERTS_PER_GROUP), TOP_K)[0].sum(-1)
    best = jnp.argmax(group_score, axis=-1)
    in_group = (jnp.arange(N_EXPERTS) // EXPERTS_PER_GROUP)[None, :] == best[:, None]
    _, eidx = lax.top_k(jnp.where(in_group, sel, -jnp.inf), TOP_K)
    gate = jnp.take_along_axis(s, eidx, axis=-1)
    gate = gate / jnp.sum(gate, -1, keepdims=True)
    comb = jnp.sum(jax.nn.one_hot(eidx, N_EXPERTS, dtype=f32) * gate[..., None], axis=1)
    y = jnp.zeros(hf.shape, f32)
    for e in range(N_EXPERTS):
        he = (jax.nn.silu(hf @ w_gate[e]) * (hf @ w_up[e])) @ w_down[e]
        y = y + comb[:, e:e + 1] * he.astype(f32)
    return y.astype(h.dtype).reshape(bsz, t, d)


def setup_inputs(seed: int = 0) -> dict:
    key = jax.random.key(seed)
    ks = list(jax.random.split(key, 32))
    f32 = jnp.float32
    nrm = lambda k, shape, scale: jax.random.normal(k, shape, f32) * scale
    uni = lambda k, shape, lo, hi: jax.random.uniform(k, shape, f32, lo, hi)
    x = nrm(ks[0], (BATCH, SEQ, D_MODEL), 1.0)
    bv0 = A_COLS + 2 * B_KEY_WIDTH
    even_scale = (jnp.ones((EVEN_COLS,), f32).at[2 * A_WIDTH:3 * A_WIDTH].set(DN_BETA)
                  .at[bv0:bv0 + B_VAL_WIDTH].set(DN_BETA))
    w_in_even = nrm(ks[1], (N_EVEN, D_MODEL, EVEN_COLS), D_MODEL ** -0.5) * even_scale
    a_mu = uni(ks[2], (N_EVEN, A_COLS), 0.0, 1.0)
    a_w0 = uni(ks[3], (N_EVEN, A_WIDTH), -6.0, 1.0)
    a_w2 = nrm(ks[4], (N_EVEN, A_DECAY_LORA, A_WIDTH), A_DECAY_LORA ** -0.5)
    a_a0 = nrm(ks[5], (N_EVEN, A_WIDTH), 0.5)
    a_a2 = nrm(ks[6], (N_EVEN, A_ICLR_LORA, A_WIDTH), 0.5 * A_ICLR_LORA ** -0.5)
    a_g2 = nrm(ks[7], (N_EVEN, A_GATE_LORA, A_WIDTH), A_GATE_LORA ** -0.5)
    a_kk_scale = 1.0 + nrm(ks[8], (N_EVEN, A_WIDTH), 0.1)
    a_ka_scale = 1.0 + nrm(ks[9], (N_EVEN, A_WIDTH), 0.1)
    a_r_k = nrm(ks[10], (N_EVEN, A_HEADS, A_HEAD_DIM), 0.1)
    a_gn_g = 1.0 + nrm(ks[11], (N_EVEN, A_WIDTH), 0.02)
    a_gn_b = nrm(ks[12], (N_EVEN, A_WIDTH), 0.02)
    b_gate_w2 = nrm(ks[13], (N_EVEN, B_GATE_LORA, B_KEY_WIDTH), B_GATE_LORA ** -0.5)
    b_gate_b = nrm(ks[14], (N_EVEN, B_KEY_WIDTH), 0.1)
    b_norm_g = 1.0 + nrm(ks[15], (N_EVEN, B_VAL_WIDTH), 0.02)
    w_out_even = nrm(ks[16], (N_EVEN, EVEN_OUT, D_MODEL), DN_BETA * EVEN_OUT ** -0.5)
    cv0 = C_WIDTH + C_HEAD_DIM
    odd_scale = jnp.ones((ODD_COLS,), f32).at[cv0:cv0 + C_HEAD_DIM].set(DN_BETA)
    w_in_odd = nrm(ks[17], (N_ODD, D_MODEL, ODD_COLS), D_MODEL ** -0.5) * odd_scale
    c_ik_ln_g = 1.0 + nrm(ks[18], (N_ODD, C_IDX_DIM), 0.02)
    c_ik_ln_b = nrm(ks[19], (N_ODD, C_IDX_DIM), 0.02)
    w_out_odd = nrm(ks[20], (N_ODD, C_WIDTH, D_MODEL), DN_BETA * C_WIDTH ** -0.5)
    ln1_g = 1.0 + nrm(ks[21], (DEPTH, D_MODEL), 0.02)
    ln1_b = nrm(ks[22], (DEPTH, D_MODEL), 0.02)
    ln2_g = 1.0 + nrm(ks[23], (DEPTH, D_MODEL), 0.02)
    ln2_b = nrm(ks[24], (DEPTH, D_MODEL), 0.02)
    router_w = nrm(ks[25], (D_MODEL, N_EXPERTS), D_MODEL ** -0.5)
    router_bias = nrm(ks[26], (N_EXPERTS,), 0.01)
    exp_w_gate = nrm(ks[27], (DEPTH, N_EXPERTS, D_MODEL, D_EXPERT), DN_BETA * D_MODEL ** -0.5)
    exp_w_up = nrm(ks[28], (DEPTH, N_EXPERTS, D_MODEL, D_EXPERT), DN_BETA * D_MODEL ** -0.5)
    exp_w_down = nrm(ks[29], (DEPTH, N_EXPERTS, D_EXPERT, D_MODEL), DN_BETA * D_EXPERT ** -0.5)
    return {'x': x, 'w_in_even': w_in_even, 'a_mu': a_mu, 'a_w0': a_w0, 'a_w2': a_w2,
            'a_a0': a_a0, 'a_a2': a_a2, 'a_g2': a_g2, 'a_kk_scale': a_kk_scale,
            'a_ka_scale': a_ka_scale, 'a_r_k': a_r_k, 'a_gn_g': a_gn_g, 'a_gn_b': a_gn_b,
            'b_gate_w2': b_gate_w2, 'b_gate_b': b_gate_b, 'b_norm_g': b_norm_g,
            'w_out_even': w_out_even, 'w_in_odd': w_in_odd, 'c_ik_ln_g': c_ik_ln_g,
            'c_ik_ln_b': c_ik_ln_b, 'w_out_odd': w_out_odd, 'ln1_g': ln1_g, 'ln1_b': ln1_b,
            'ln2_g': ln2_g, 'ln2_b': ln2_b, 'router_w': router_w, 'router_bias': router_bias,
            'exp_w_gate': exp_w_gate, 'exp_w_up': exp_w_up, 'exp_w_down': exp_w_down}


def reference(x, w_in_even, a_mu, a_w0, a_w2, a_a0, a_a2, a_g2, a_kk_scale, a_ka_scale, a_r_k,
              a_gn_g, a_gn_b, b_gate_w2, b_gate_b, b_norm_g, w_out_even, w_in_odd, c_ik_ln_g,
              c_ik_ln_b, w_out_odd, ln1_g, ln1_b, ln2_g, ln2_b, router_w, router_bias,
              exp_w_gate, exp_w_up, exp_w_down):
    k_top = min(C_INDEX_TOPK, x.shape[1] // 4)
    for l in range(DEPTH):
        i = l // 2
        if l % 2 == 0:
            p = x @ w_in_even[i]
            ya = _rwkv7_group(p[..., :A_COLS], a_mu[i], a_w0[i], a_w2[i], a_a0[i], a_a2[i],
                              a_g2[i], a_kk_scale[i], a_ka_scale[i], a_r_k[i], a_gn_g[i], a_gn_b[i])
            yb = _gla_group(p[..., A_COLS:], b_gate_w2[i], b_gate_b[i], b_norm_g[i])
            mix = jnp.concatenate([ya, yb], axis=-1) @ w_out_even[i]
        else:
            p = x @ w_in_odd[i]
            mix = _dsa_mixer(p, c_ik_ln_g[i], c_ik_ln_b[i], k_top) @ w_out_odd[i]
        h = _layer_norm(DN_ALPHA * x + mix, ln1_g[l], ln1_b[l])
        ffn = _moe(h, router_w, router_bias, exp_w_gate[l], exp_w_up[l], exp_w_down[l])
        x = _layer_norm(DN_ALPHA * h + ffn, ln2_g[l], ln2_b[l])
    return x
```

```python
import functools
import math

import numpy as np
import jax
import jax.numpy as jnp
from jax import lax
from jax.experimental import pallas as pl
from jax.experimental.pallas import tpu as pltpu

F32 = jnp.float32
BF16 = jnp.bfloat16
I32 = jnp.int32

D_MODEL = 1024
DEPTH = 2
A_HEADS, A_HEAD_DIM = 8, 64
A_WIDTH = A_HEADS * A_HEAD_DIM
A_DECAY_LORA, A_ICLR_LORA, A_GATE_LORA = 64, 64, 128
A_GN_EPS = 64e-5
A_COLS = 3 * A_WIDTH + A_DECAY_LORA + A_ICLR_LORA + A_GATE_LORA
B_HEADS, B_KEY_DIM, B_VAL_DIM = 4, 64, 128
B_KEY_WIDTH = B_HEADS * B_KEY_DIM
B_VAL_WIDTH = B_HEADS * B_VAL_DIM
B_GATE_LORA = 16
B_GATE_TAU = 16.0
B_NORM_EPS = 1e-5
B_COLS = 2 * B_KEY_WIDTH + 2 * B_VAL_WIDTH + B_GATE_LORA
C_HEADS, C_HEAD_DIM = 8, 128
C_WIDTH = C_HEADS * C_HEAD_DIM
C_IDX_HEADS, C_IDX_DIM = 4, 64
C_INDEX_TOPK = 256
ODD_COLS = C_WIDTH + 2 * C_HEAD_DIM + C_IDX_HEADS * C_IDX_DIM + C_IDX_DIM + C_IDX_HEADS
ROPE_THETA = 10000.0
N_EXPERTS, N_GROUPS, TOP_K, D_EXPERT = 16, 4, 2, 256
EXPERTS_PER_GROUP = N_EXPERTS // N_GROUPS
DN_ALPHA = (2 * DEPTH) ** 0.25
LN_EPS = 1e-5

LANE = 128
CHUNK = 64
INV_BLOCK = 16
VMEM_LIMIT = 56 * 1024 * 1024
B_PAD_COLS = 13 * LANE
ODD_PAD_COLS = 13 * LANE
INT_MIN = -2 ** 31
NEG_BIG = -1e30
TIE_NONE = 2 ** 30


def _cparams(*sem):
    return pltpu.CompilerParams(dimension_semantics=sem, vmem_limit_bytes=VMEM_LIMIT)


def _dot(a, b):
    return jnp.dot(a.astype(BF16), b.astype(BF16), preferred_element_type=F32)


def _dot_nt(a, b):
    return lax.dot_general(a.astype(BF16), b.astype(BF16), (((1,), (1,)), ((), ())),
                           preferred_element_type=F32)


def _dot_tn(a, b):
    return lax.dot_general(a.astype(BF16), b.astype(BF16), (((0,), (0,)), ((), ())),
                           preferred_element_type=F32)


def _split2(a):
    hi = a.astype(BF16)
    lo = (a - hi.astype(F32)).astype(BF16)
    return hi, lo


def _split3(a):
    hi = a.astype(BF16)
    r1 = a - hi.astype(F32)
    mid = r1.astype(BF16)
    lo = (r1 - mid.astype(F32)).astype(BF16)
    return hi, mid, lo


def _dot_exact_lhs(l_bf16, a):
    hi, mid, lo = _split3(a)
    d = lambda z: jnp.dot(l_bf16, z, preferred_element_type=F32)
    return d(hi) + d(mid) + d(lo)


def _dot_exact_rhs(a, r_bf16):
    hi, mid, lo = _split3(a)
    d = lambda z: jnp.dot(z, r_bf16, preferred_element_type=F32)
    return d(hi) + d(mid) + d(lo)


def _dot3(a, b):
    ah, al = _split2(a)
    bh, bl = _split2(b)
    d = lambda x, y: jnp.dot(x, y, preferred_element_type=F32)
    return d(ah, bh) + d(ah, bl) + d(al, bh)


def _dot3_nt(a, b):
    ah, al = _split2(a)
    bh, bl = _split2(b)
    d = lambda x, y: lax.dot_general(x, y, (((1,), (1,)), ((), ())), preferred_element_type=F32)
    return d(ah, bh) + d(ah, bl) + d(al, bh)


def _sigmoid(z):
    return 1.0 / (1.0 + jnp.exp(-z))


def _layer_norm(z, g, b):
    mu = jnp.mean(z, axis=-1, keepdims=True)
    zc = z - mu
    var = jnp.mean(zc * zc, axis=-1, keepdims=True)
    return zc * lax.rsqrt(var + LN_EPS) * g + b


def _proj_kernel(x_ref, *refs):
    n_out = len(refs) // 2
    xb = x_ref[...].astype(BF16)
    for w_ref, o_ref in zip(refs[:n_out], refs[n_out:]):
        o_ref[...] = jnp.dot(xb, w_ref[...], preferred_element_type=F32)


def _proj(x2d, ws, tm=256):
    n, d = x2d.shape
    in_specs = [pl.BlockSpec((tm, d), lambda i: (i, 0))]
    in_specs += [pl.BlockSpec(w.shape, lambda i: (0, 0)) for w in ws]
    out_specs = [pl.BlockSpec((tm, w.shape[1]), lambda i: (i, 0)) for w in ws]
    out_shape = [jax.ShapeDtypeStruct((n, w.shape[1]), F32) for w in ws]
    return pl.pallas_call(
        _proj_kernel, grid=(n // tm,), in_specs=in_specs, out_specs=out_specs,
        out_shape=out_shape, compiler_params=_cparams("parallel"), name="in_proj",
    )(x2d, *ws)


def _rwkv_prep_kernel(p_ref, mu_ref, wwa_ref, w0_ref, a0_ref, g2_ref, kks_ref, kas_ref, hsum_ref,
                      r_ref, lw_ref, k_ref, v_ref, kk_ref, ab_ref, g_ref, carry_ref):
    t = pl.program_id(1)

    @pl.when(t == 0)
    def _():
        carry_ref[...] = jnp.zeros_like(carry_ref)

    p = p_ref[0]
    tm = p.shape[0]
    prev = carry_ref[...]
    row = lax.broadcasted_iota(I32, p.shape, 0)
    shifted = jnp.where(row == 0, prev, pltpu.roll(p, 1, axis=0))
    carry_ref[...] = p[tm - 1:tm, :]
    pm = p + (shifted - p) * mu_ref[...]

    w = A_WIDTH
    r = pm[:, 0:w]
    k = pm[:, w:2 * w]
    v = pm[:, 2 * w:3 * w]
    xwa = pm[:, 3 * w:3 * w + LANE]
    xg = pm[:, 3 * w + LANE:3 * w + 2 * LANE]

    lane = lax.broadcasted_iota(I32, xwa.shape, 1)
    lhs = jnp.where(lane < A_DECAY_LORA, jnp.tanh(xwa), xwa)
    z = _dot3(lhs, wwa_ref[...])
    lw = -_sigmoid(z[:, 0:w] + w0_ref[...]) * math.exp(-0.5)
    a = _sigmoid(z[:, w:2 * w] + a0_ref[...])
    g = _dot(_sigmoid(xg), g2_ref[...])

    kks = k * kks_ref[...]
    ss = _dot_exact_rhs(kks * kks, hsum_ref[...])
    kkn = kks * lax.rsqrt(jnp.maximum(ss, 1e-24))
    kh = k * (1.0 + (a - 1.0) * kas_ref[...])

    r_ref[0] = r
    lw_ref[0] = lw
    k_ref[0] = kh
    v_ref[0] = v
    kk_ref[0] = kkn
    ab_ref[0] = kkn * a
    g_ref[0] = g


def _rwkv_prep(pa, mu, wwa, w0, a0, g2, kks, kas, hsum, tm=256):
    b, t, _ = pa.shape
    w = A_WIDTH
    row = lambda z: z.reshape(1, -1)
    full = lambda z: pl.BlockSpec(z.shape, lambda i, j: (0,) * z.ndim)
    args = [pa, row(mu), wwa, row(w0), row(a0), g2, row(kks), row(kas), hsum]
    in_specs = [pl.BlockSpec((1, tm, A_COLS), lambda i, j: (i, j, 0))] + [full(z) for z in args[1:]]
    out_spec = pl.BlockSpec((1, tm, w), lambda i, j: (i, j, 0))
    return pl.pallas_call(
        _rwkv_prep_kernel, grid=(b, t // tm), in_specs=in_specs, out_specs=[out_spec] * 7,
        out_shape=[jax.ShapeDtypeStruct((b, t, w), F32)] * 7,
        scratch_shapes=[pltpu.VMEM((1, A_COLS), F32)],
        compiler_params=_cparams("parallel", "arbitrary"), name="rwkv_prep",
    )(*args)


def _tri_inv(a, blk_mask, eye_f):
    dg = jnp.where(blk_mask, a, 0.0)
    e = a - dg
    x = -dg
    dinv = eye_f + x
    pw = x
    for _ in range(int(math.log2(INV_BLOCK)) - 1):
        pw = _dot(pw, pw)
        dinv = _dot(dinv, eye_f + pw)
    f = _dot(dinv, e)
    m = dinv - _dot(f, dinv)
    n_blk = a.shape[0] // INV_BLOCK
    assert n_blk == 4
    return m + _dot(_dot(f, f), m)


def _rwkv_core_kernel(r_ref, lw_ref, k_ref, v_ref, kk_ref, ab_ref, g_ref, rk_ref, gng_ref, gnb_ref,
                      o_ref, s_ref):
    t = pl.program_id(1)

    @pl.when(t == 0)
    def _():
        s_ref[...] = jnp.zeros_like(s_ref)

    c = CHUNK
    hd = A_HEAD_DIM
    n_chunk = r_ref.shape[1] // c
    ri = lax.broadcasted_iota(I32, (c, c), 0)
    ci = lax.broadcasted_iota(I32, (c, c), 1)
    tri_incl = jnp.where(ci <= ri, 1.0, 0.0).astype(BF16)
    eye_f = jnp.where(ci == ri, 1.0, 0.0).astype(F32)
    blk_mask = (ri // INV_BLOCK) == (ci // INV_BLOCK)
    gi = lax.broadcasted_iota(I32, (2 * c, 2 * c), 0)
    gj = lax.broadcasted_iota(I32, (2 * c, 2 * c), 1)
    g_mask = (gj & (c - 1)) < (gi & (c - 1)) + jnp.where(gi >= c, 1, 0)

    def chunk_body(ic, carry):
        sl = pl.ds(pl.multiple_of(ic * c, c), c)
        r = r_ref[0, sl, :]
        lw = lw_ref[0, sl, :]
        k = k_ref[0, sl, :]
        v = v_ref[0, sl, :]
        kk = kk_ref[0, sl, :]
        ab = ab_ref[0, sl, :]
        g = g_ref[0, sl, :]
        cum = _dot_exact_lhs(tri_incl, lw)
        cl = cum[c - 1:c, :]
        e_neg = jnp.exp(-cum)
        e_end = jnp.exp(cl - cum)
        rq = r * jnp.exp(cum)
        kq = kk * jnp.exp(cum - lw)
        bd = ab * e_neg
        kd = k * e_neg
        be = ab * e_end
        ke = k * e_end
        p_end = jnp.exp(cl)
        rkk = r * k * rk_ref[...]

        for h in range(A_HEADS):
            hs = slice(h * hd, (h + 1) * hd)
            kq_h, rq_h, v_h = kq[:, hs], rq[:, hs], v[:, hs]
            gm = jnp.where(g_mask,
                           _dot_nt(jnp.concatenate([kq_h, rq_h], axis=0),
                                   jnp.concatenate([bd[:, hs], kd[:, hs]], axis=0)), 0.0)
            a_ab, a_ak = gm[:c, :c], gm[:c, c:]
            a_rb, a_rk = gm[c:, :c], gm[c:, c:]
            tinv = _tri_inv(a_ab, blk_mask, eye_f)
            wt = -_dot(tinv, kq_h)
            ut = -_dot(tinv, _dot(a_ak, v_h))
            s0 = s_ref[h]
            u = _dot_nt(wt, s0) + ut
            y = _dot_nt(rq_h, s0) + _dot(a_rb, u) + _dot(a_rk, v_h)
            s_ref[h] = (s0 * p_end[:, hs]
                        + _dot_tn(jnp.concatenate([u, v_h], axis=0),
                                  jnp.concatenate([be[:, hs], ke[:, hs]], axis=0)))
            ym = jnp.mean(y, axis=-1, keepdims=True)
            yc = y - ym
            yv = jnp.mean(yc * yc, axis=-1, keepdims=True)
            yn = yc * lax.rsqrt(yv + A_GN_EPS) * gng_ref[:, hs] + gnb_ref[:, hs]
            bonus = jnp.sum(rkk[:, hs], axis=-1, keepdims=True) * v_h
            o_ref[0, sl, hs] = (yn + bonus) * g[:, hs]
        return carry

    lax.fori_loop(0, n_chunk, chunk_body, 0)


def _rwkv_core(r, lw, k, v, kk, ab, g, r_k, gn_g, gn_b, tb=256):
    b, t, w = r.shape
    row = lambda z: z.reshape(1, -1)
    blk = pl.BlockSpec((1, tb, w), lambda i, j: (i, j, 0))
    par = pl.BlockSpec((1, w), lambda i, j: (0, 0))
    return pl.pallas_call(
        _rwkv_core_kernel, grid=(b, t // tb), in_specs=[blk] * 7 + [par] * 3, out_specs=blk,
        out_shape=jax.ShapeDtypeStruct((b, t, w), F32),
        scratch_shapes=[pltpu.VMEM((A_HEADS, A_HEAD_DIM, A_HEAD_DIM), F32)],
        compiler_params=_cparams("parallel", "arbitrary"), name="rwkv_core",
    )(r, lw, k, v, kk, ab, g, row(r_k), row(gn_g), row(gn_b))


def _gla_kernel(p_ref, gw2_ref, gb_ref, ng_ref, o_ref, s_ref):
    t = pl.program_id(1)

    @pl.when(t == 0)
    def _():
        s_ref[...] = jnp.zeros_like(s_ref)

    c = CHUNK
    kw, vw = B_KEY_WIDTH, B_VAL_WIDTH
    n_chunk = p_ref.shape[1] // c
    ri = lax.broadcasted_iota(I32, (c, c), 0)
    ci = lax.broadcasted_iota(I32, (c, c), 1)
    causal = ci <= ri
    tri_incl = jnp.where(causal, 1.0, 0.0).astype(BF16)

    def chunk_body(ic, carry):
        sl = pl.ds(pl.multiple_of(ic * c, c), c)
        q = p_ref[0, sl, 0:kw] * (B_KEY_DIM ** -0.5)
        k = p_ref[0, sl, kw:2 * kw]
        v = p_ref[0, sl, 2 * kw:2 * kw + vw]
        g = p_ref[0, sl, 2 * kw + vw:2 * kw + 2 * vw]
        xa = p_ref[0, sl, 2 * kw + 2 * vw:2 * kw + 2 * vw + LANE]
        z = _dot3(xa, gw2_ref[...]) + gb_ref[...]
        log_a = (jnp.minimum(z, 0.0) - jnp.log(1.0 + jnp.exp(-jnp.abs(z)))) * (1.0 / B_GATE_TAU)
        bc = _dot_exact_lhs(tri_incl, log_a)
        bl = bc[c - 1:c, :]
        q_dec = q * jnp.exp(bc)
        k_inv = k * jnp.exp(-bc)
        k_end = k * jnp.exp(bl - bc)
        p_end = jnp.exp(bl)
        for h in range(B_HEADS):
            ks = slice(h * B_KEY_DIM, (h + 1) * B_KEY_DIM)
            vs = slice(h * B_VAL_DIM, (h + 1) * B_VAL_DIM)
            v_h = v[:, vs]
            att = jnp.where(causal, _dot_nt(q_dec[:, ks], k_inv[:, ks]), 0.0)
            s0 = s_ref[h]
            o = _dot(att, v_h) + _dot_nt(q_dec[:, ks], s0)
            s_ref[h] = s0 * p_end[:, ks] + _dot_tn(v_h, k_end[:, ks])
            o = o * lax.rsqrt(jnp.mean(o * o, axis=-1, keepdims=True) + B_NORM_EPS)
            g_h = g[:, vs]
            o_ref[0, sl, vs] = o * ng_ref[:, vs] * (g_h * _sigmoid(g_h))
        return carry

    lax.fori_loop(0, n_chunk, chunk_body, 0)


def _gla(pb, gw2p, gate_b, norm_g, tb=256):
    b, t, cols = pb.shape
    return pl.pallas_call(
        _gla_kernel, grid=(b, t // tb),
        in_specs=[pl.BlockSpec((1, tb, cols), lambda i, j: (i, j, 0)),
                  pl.BlockSpec(gw2p.shape, lambda i, j: (0, 0)),
                  pl.BlockSpec((1, B_KEY_WIDTH), lambda i, j: (0, 0)),
                  pl.BlockSpec((1, B_VAL_WIDTH), lambda i, j: (0, 0))],
        out_specs=pl.BlockSpec((1, tb, B_VAL_WIDTH), lambda i, j: (i, j, 0)),
        out_shape=jax.ShapeDtypeStruct((b, t, B_VAL_WIDTH), F32),
        scratch_shapes=[pltpu.VMEM((B_HEADS, B_VAL_DIM, B_KEY_DIM), F32)],
        compiler_params=_cparams("parallel", "arbitrary"), name="gla",
    )(pb, gw2p, gate_b.reshape(1, -1), norm_g.reshape(1, -1))


def _outproj_ln_kernel(*refs):
    n_in = (len(refs) - 4) // 2
    x_ref, g_ref, b_ref, o_ref = refs[2 * n_in:]
    mix = None
    for y_ref, w_ref in zip(refs[:n_in], refs[n_in:2 * n_in]):
        d = jnp.dot(y_ref[...].astype(BF16), w_ref[...], preferred_element_type=F32)
        mix = d if mix is None else mix + d
    o_ref[...] = _layer_norm(DN_ALPHA * x_ref[...] + mix, g_ref[...], b_ref[...])


def _outproj_ln(ys, ws, x2d, g, b, tm=256):
    n, d = x2d.shape
    in_specs = [pl.BlockSpec((tm, y.shape[1]), lambda i: (i, 0)) for y in ys]
    in_specs += [pl.BlockSpec(w.shape, lambda i: (0, 0)) for w in ws]
    in_specs += [pl.BlockSpec((tm, d), lambda i: (i, 0)),
                 pl.BlockSpec((1, d), lambda i: (0, 0)), pl.BlockSpec((1, d), lambda i: (0, 0))]
    return pl.pallas_call(
        _outproj_ln_kernel, grid=(n // tm,), in_specs=in_specs,
        out_specs=pl.BlockSpec((tm, d), lambda i: (i, 0)),
        out_shape=jax.ShapeDtypeStruct((n, d), F32),
        compiler_params=_cparams("parallel"), name="out_proj_ln",
    )(*ys, *ws, x2d, g.reshape(1, -1), b.reshape(1, -1))


def _router_kernel(h_ref, rwt_ref, rb_ref, o_ref):
    logits = _dot3_nt(rwt_ref[...], h_ref[...])
    s = _sigmoid(logits)
    sel = s + rb_ref[...]
    s_rows = [s[e:e + 1, :] for e in range(N_EXPERTS)]
    rows = [sel[e:e + 1, :] for e in range(N_EXPERTS)]
    best_val, best = None, None
    for gidx in range(N_GROUPS):
        mem = rows[gidx * EXPERTS_PER_GROUP:(gidx + 1) * EXPERTS_PER_GROUP]
        gs = None
        for i in range(EXPERTS_PER_GROUP):
            for j in range(i + 1, EXPERTS_PER_GROUP):
                pair = mem[i] + mem[j]
                gs = pair if gs is None else jnp.maximum(gs, pair)
        if best_val is None:
            best_val, best = gs, jnp.zeros(gs.shape, I32)
        else:
            upd = gs > best_val
            best = jnp.where(upd, gidx, best)
            best_val = jnp.where(upd, gs, best_val)
    vals = [jnp.where(best == (e // EXPERTS_PER_GROUP), rows[e], -jnp.inf) for e in range(N_EXPERTS)]

    def arg_top(vs):
        m = functools.reduce(jnp.maximum, vs)
        idx = jnp.full(m.shape, N_EXPERTS, I32)
        for e in reversed(range(N_EXPERTS)):
            idx = jnp.where(vs[e] == m, e, idx)
        return idx

    i1 = arg_top(vals)
    i2 = arg_top([jnp.where(i1 == e, -jnp.inf, vals[e]) for e in range(N_EXPERTS)])
    g1 = functools.reduce(jnp.add, [jnp.where(i1 == e, s_rows[e], 0.0) for e in range(N_EXPERTS)])
    g2 = functools.reduce(jnp.add, [jnp.where(i2 == e, s_rows[e], 0.0) for e in range(N_EXPERTS)])
    tot = g1 + g2
    for e in range(N_EXPERTS):
        o_ref[e:e + 1, :] = jnp.where(i1 == e, g1 / tot, 0.0) + jnp.where(i2 == e, g2 / tot, 0.0)


def _router(h2d, rwt, rbias, tm=256):
    n, d = h2d.shape
    return pl.pallas_call(
        _router_kernel, grid=(n // tm,),
        in_specs=[pl.BlockSpec((tm, d), lambda i: (i, 0)),
                  pl.BlockSpec((N_EXPERTS, d), lambda i: (0, 0)),
                  pl.BlockSpec((N_EXPERTS, 1), lambda i: (0, 0))],
        out_specs=pl.BlockSpec((N_EXPERTS, tm), lambda i: (0, i)),
        out_shape=jax.ShapeDtypeStruct((N_EXPERTS, n), F32),
        compiler_params=_cparams("parallel"), name="router",
    )(h2d, rwt, rbias.reshape(-1, 1))


def _moe_kernel(h_ref, comb_ref, wgu_ref, wd_ref, g_ref, b_ref, o_ref, acc_ref, hb_ref):
    e = pl.program_id(1)

    @pl.when(e == 0)
    def _():
        acc_ref[...] = jnp.zeros_like(acc_ref)
        hb_ref[...] = h_ref[...].astype(BF16)

    gu = jnp.dot(hb_ref[...], wgu_ref[0], preferred_element_type=F32)
    gt, up = gu[:, :D_EXPERT], gu[:, D_EXPERT:]
    comb = comb_ref[...]
    lane = lax.broadcasted_iota(I32, comb.shape, 1)
    ce = jnp.sum(jnp.where(lane == e, comb, 0.0), axis=-1, keepdims=True)
    act = (gt * _sigmoid(gt)) * up
    he = jnp.dot(act.astype(BF16), wd_ref[0], preferred_element_type=F32)
    acc_ref[...] += ce * he

    @pl.when(e == N_EXPERTS - 1)
    def _():
        o_ref[...] = _layer_norm(DN_ALPHA * h_ref[...] + acc_ref[...], g_ref[...], b_ref[...])


def _moe(h2d, comb, wgu, wd, g, b, tm=512):
    n, d = h2d.shape
    return pl.pallas_call(
        _moe_kernel, grid=(n // tm, N_EXPERTS),
        in_specs=[pl.BlockSpec((tm, d), lambda i, e: (i, 0)),
                  pl.BlockSpec((tm, N_EXPERTS), lambda i, e: (i, 0)),
                  pl.BlockSpec((1, d, 2 * D_EXPERT), lambda i, e: (e, 0, 0)),
                  pl.BlockSpec((1, D_EXPERT, d), lambda i, e: (e, 0, 0)),
                  pl.BlockSpec((1, d), lambda i, e: (0, 0)),
                  pl.BlockSpec((1, d), lambda i, e: (0, 0))],
        out_specs=pl.BlockSpec((tm, d), lambda i, e: (i, 0)),
        out_shape=jax.ShapeDtypeStruct((n, d), F32),
        scratch_shapes=[pltpu.VMEM((tm, d), F32), pltpu.VMEM((tm, d), BF16)],
        compiler_params=_cparams("parallel", "arbitrary"), name="moe",
    )(h2d, comb, wgu, wd, g.reshape(1, -1), b.reshape(1, -1))


def _rope_full(z, cos, sin_signed):
    return z * cos + pltpu.roll(z, LANE // 2, axis=1) * sin_signed


def _rope_half(z, cos, sin_signed, first_half):
    partner = jnp.where(first_half, pltpu.roll(z, LANE - C_IDX_DIM // 2, axis=1),
                        pltpu.roll(z, C_IDX_DIM // 2, axis=1))
    return z * cos + partner * sin_signed


def _dsa_prep_kernel(p_ref, c128_ref, s128_ref, c64_ref, s64_ref, lng_ref, lnb_ref,
                     q_ref, k_ref, v_ref, iq_ref, ika_ref, ikb_ref, iwb_ref):
    c128, s128 = c128_ref[...], s128_ref[...]
    c64, s64 = c64_ref[...], s64_ref[...]
    lane = lax.broadcasted_iota(I32, c64.shape, 1)
    first_half = (lane & (C_IDX_DIM - 1)) < C_IDX_DIM // 2
    for h in range(C_HEADS):
        hs = slice(h * LANE, (h + 1) * LANE)
        q_ref[0, :, hs] = (_rope_full(p_ref[0, :, hs], c128, s128) * C_HEAD_DIM ** -0.5).astype(BF16)
    k0 = C_WIDTH
    k_ref[0] = _rope_full(p_ref[0, :, k0:k0 + LANE], c128, s128).astype(BF16)
    v_ref[0] = p_ref[0, :, k0 + LANE:k0 + 2 * LANE].astype(BF16)
    i0 = k0 + 2 * LANE
    for j in range(C_IDX_HEADS * C_IDX_DIM // LANE):
        js = slice(j * LANE, (j + 1) * LANE)
        z = p_ref[0, :, i0 + j * LANE:i0 + (j + 1) * LANE]
        iq_ref[0, :, js] = (_rope_half(z, c64, s64, first_half) * C_IDX_DIM ** -0.5).astype(BF16)
    t0 = i0 + C_IDX_HEADS * C_IDX_DIM
    tile = p_ref[0, :, t0:t0 + LANE]
    is_key = lane < C_IDX_DIM
    mu = jnp.sum(jnp.where(is_key, tile, 0.0), axis=-1, keepdims=True) * (1.0 / C_IDX_DIM)
    zc = jnp.where(is_key, tile - mu, 0.0)
    var = jnp.sum(zc * zc, axis=-1, keepdims=True) * (1.0 / C_IDX_DIM)
    ikn = zc * lax.rsqrt(var + LN_EPS) * lng_ref[...] + lnb_ref[...]
    ikr = jnp.where(is_key, _rope_half(ikn, c64, s64, first_half), 0.0)
    ika_ref[0] = ikr.astype(BF16)
    ikb_ref[0] = pltpu.roll(ikr, C_IDX_DIM, axis=1).astype(BF16)
    for h in range(C_IDX_HEADS):
        col = tile[:, C_IDX_DIM + h:C_IDX_DIM + h + 1] * C_IDX_HEADS ** -0.5
        iwb_ref[0, :, h * LANE:(h + 1) * LANE] = jnp.broadcast_to(col, tile.shape)


def _dsa_prep(p, tabs, lng, lnb, tm=256):
    b, t, cols = p.shape
    tab = pl.BlockSpec((tm, LANE), lambda i, j: (j, 0))
    par = pl.BlockSpec((1, LANE), lambda i, j: (0, 0))
    o = lambda w: pl.BlockSpec((1, tm, w), lambda i, j: (i, j, 0))
    widths = [C_WIDTH, LANE, LANE, C_IDX_HEADS * C_IDX_DIM, LANE, LANE, C_IDX_HEADS * LANE]
    dtypes = [BF16, BF16, BF16, BF16, BF16, BF16, F32]
    return pl.pallas_call(
        _dsa_prep_kernel, grid=(b, t // tm),
        in_specs=[pl.BlockSpec((1, tm, cols), lambda i, j: (i, j, 0)), tab, tab, tab, tab, par, par],
        out_specs=[o(w) for w in widths],
        out_shape=[jax.ShapeDtypeStruct((b, t, w), dt) for w, dt in zip(widths, dtypes)],
        compiler_params=_cparams("parallel", "parallel"), name="dsa_prep",
    )(p, *tabs, lng, lnb)


def _dsa_attn_kernel(q_ref, iq_ref, iwb_ref, k_ref, v_ref, ika_ref, ikb_ref, o_ref,
                     key_ref, tie_ref, *, ktop, idx_bits):
    i = pl.program_id(1)
    tq = q_ref.shape[1]
    nk = i + 1
    assert tq == LANE
    lane = lax.broadcasted_iota(I32, (tq, LANE), 1)
    rowpos = i * tq + lax.broadcasted_iota(I32, (tq, LANE), 0)

    iq = iq_ref[0]
    lhs_i = jnp.concatenate([iq[:, :LANE], iq[:, LANE:]], axis=0)
    w_h = [iwb_ref[0, :, h * LANE:(h + 1) * LANE] for h in range(C_IDX_HEADS)]

    def score_body(kb, carry):
        sa = _dot_nt(lhs_i, ika_ref[0, kb])
        sb = _dot_nt(lhs_i, ikb_ref[0, kb])
        relu = lambda z: jnp.maximum(z, 0.0)
        sc = (relu(sa[:tq]) * w_h[0] + relu(sb[:tq]) * w_h[1]
              + relu(sa[tq:]) * w_h[2] + relu(sb[tq:]) * w_h[3]) + 0.0
        bits = pltpu.bitcast(sc, I32)
        skey = bits ^ ((bits >> 31) & 0x7FFFFFFF)
        key_ref[kb] = jnp.where(kb * LANE + lane <= rowpos, skey, INT_MIN)
        return carry

    lax.fori_loop(0, nk, score_body, 0)

    def count(ref, pred):
        def body(kb, acc):
            return acc + jnp.where(pred(ref[kb]), 1, 0)
        acc = lax.fori_loop(0, nk, body, jnp.zeros((tq, LANE), I32))
        return jnp.broadcast_to(jnp.sum(acc, axis=1, keepdims=True), (tq, LANE))

    c0 = count(key_ref, lambda x: x >= 0)
    tau = jnp.where(c0 >= ktop, 0, INT_MIN).astype(I32)

    def bisect(it, tau):
        cand = tau + lax.shift_left(jnp.int32(1), 30 - it)
        cnt = count(key_ref, lambda x: x >= cand)
        return jnp.where(cnt >= ktop, cand, tau)

    tau = lax.fori_loop(0, 31, bisect, tau)
    need = ktop - count(key_ref, lambda x: x > tau)

    def tie_body(kb, carry):
        key = key_ref[kb]
        idx = jnp.where(key == INT_MIN, TIE_NONE, kb * LANE + lane)
        tie_ref[kb] = jnp.where(key == tau, idx, TIE_NONE)
        return carry

    lax.fori_loop(0, nk, tie_body, 0)

    def bisect_idx(it, ans):
        cand = ans + lax.shift_left(jnp.int32(1), idx_bits - 1 - it)
        cnt = count(tie_ref, lambda x: x < cand)
        return jnp.where(cnt < need, cand, ans)

    jstar = lax.fori_loop(0, idx_bits, bisect_idx, jnp.zeros((tq, LANE), I32))

    q = q_ref[0]
    qs = jnp.concatenate([q[:, h * LANE:(h + 1) * LANE] for h in range(C_HEADS)], axis=0)
    rows = C_HEADS * tq

    def att_body(kb, carry):
        m, l, acc = carry
        s = _dot_nt(qs, k_ref[0, kb])
        sel1 = jnp.where(key_ref[kb] > tau, 1, jnp.where(tie_ref[kb] <= jstar, 1, 0))
        sel = jnp.concatenate([sel1] * C_HEADS, axis=0) > 0
        s = jnp.where(sel, s, NEG_BIG)
        m_new = jnp.maximum(m, jnp.max(s, axis=-1, keepdims=True))
        alpha = jnp.exp(m - m_new)
        p = jnp.where(sel, jnp.exp(s - m_new), 0.0)
        l = alpha * l + jnp.sum(p, axis=-1, keepdims=True)
        acc = acc * alpha + jnp.dot(p.astype(BF16), v_ref[0, kb], preferred_element_type=F32)
        return m_new, l, acc

    m0 = jnp.full((rows, 1), NEG_BIG, F32)
    l0 = jnp.zeros((rows, 1), F32)
    a0 = jnp.zeros((rows, LANE), F32)
    _, l, acc = lax.fori_loop(0, nk, att_body, (m0, l0, a0))
    out = acc / l
    for h in range(C_HEADS):
        o_ref[0, :, h * LANE:(h + 1) * LANE] = out[h * tq:(h + 1) * tq]


def _dsa_attn(q, iq, iwb, k4, v4, ika4, ikb4, ktop):
    b, t, _ = q.shape
    tq = LANE
    nkb = t // LANE
    qspec = lambda w: pl.BlockSpec((1, tq, w), lambda i, j: (i, j, 0))
    kspec = pl.BlockSpec((1, nkb, LANE, LANE), lambda i, j: (i, 0, 0, 0))
    kern = functools.partial(_dsa_attn_kernel, ktop=ktop, idx_bits=int(math.log2(t)))
    return pl.pallas_call(
        kern, grid=(b, t // tq),
        in_specs=[qspec(C_WIDTH), qspec(C_IDX_HEADS * C_IDX_DIM), qspec(C_IDX_HEADS * LANE),
                  kspec, kspec, kspec, kspec],
        out_specs=qspec(C_WIDTH),
        out_shape=jax.ShapeDtypeStruct((b, t, C_WIDTH), F32),
        scratch_shapes=[pltpu.VMEM((nkb, tq, LANE), I32), pltpu.VMEM((nkb, tq, LANE), I32)],
        compiler_params=_cparams("parallel", "arbitrary"), name="dsa_attn",
    )(q, iq, iwb, k4, v4, ika4, ikb4)


def _rope_tables(t):
    pos = jnp.arange(t, dtype=F32)

    def tab(dim):
        half = dim // 2
        inv = ROPE_THETA ** (-jnp.arange(half, dtype=F32) / half)
        ang = pos[:, None] * inv[None, :]
        cos, sin = jnp.cos(ang), jnp.sin(ang)
        reps = LANE // dim
        return (jnp.tile(jnp.concatenate([cos, cos], axis=-1), (1, reps)),
                jnp.tile(jnp.concatenate([-sin, sin], axis=-1), (1, reps)))

    c128, s128 = tab(C_HEAD_DIM)
    c64, s64 = tab(C_IDX_DIM)
    return c128, s128, c64, s64


def _pad_cols(w, cols):
    return jnp.pad(w, ((0, 0), (0, cols - w.shape[1])))


def _moe_block(h, rwt, router_bias, w_gate, w_up, w_down, g, b):
    comb = _router(h, rwt, router_bias).T
    wgu = jnp.concatenate([w_gate, w_up], axis=-1).astype(BF16)
    return _moe(h, comb, wgu, w_down.astype(BF16), g, b)


@jax.jit
def _forward(x, w_in_even, a_mu, a_w0, a_w2, a_a0, a_a2, a_g2, a_kk_scale, a_ka_scale, a_r_k,
             a_gn_g, a_gn_b, b_gate_w2, b_gate_b, b_norm_g, w_out_even, w_in_odd, c_ik_ln_g,
             c_ik_ln_b, w_out_odd, ln1_g, ln1_b, ln2_g, ln2_b, router_w, router_bias,
             exp_w_gate, exp_w_up, exp_w_down):
    bsz, t, d = x.shape
    assert d == D_MODEL and t % 256 == 0 and (t & (t - 1)) == 0
    n = bsz * t
    ktop = min(C_INDEX_TOPK, t // 4)
    rwt = router_w.T
    xf = x.reshape(n, d)
    hsum = (jnp.arange(A_WIDTH)[:, None] // A_HEAD_DIM
            == jnp.arange(A_WIDTH)[None, :] // A_HEAD_DIM).astype(BF16)

    for l in range(DEPTH):
        i = l // 2
        if l % 2 == 0:
            w = w_in_even[i]
            pa, pb = _proj(xf, [w[:, :A_COLS].astype(BF16),
                                _pad_cols(w[:, A_COLS:], B_PAD_COLS).astype(BF16)])
            wwa = jnp.zeros((LANE, 2 * A_WIDTH), F32)
            wwa = wwa.at[:A_DECAY_LORA, :A_WIDTH].set(a_w2[i]).at[A_DECAY_LORA:, A_WIDTH:].set(a_a2[i])
            parts = _rwkv_prep(pa.reshape(bsz, t, A_COLS), a_mu[i], wwa, a_w0[i], a_a0[i], a_g2[i],
                               a_kk_scale[i], a_ka_scale[i], hsum)
            ya = _rwkv_core(*parts, a_r_k[i].reshape(-1), a_gn_g[i], a_gn_b[i])
            gw2p = jnp.zeros((LANE, B_KEY_WIDTH), F32).at[:B_GATE_LORA].set(b_gate_w2[i])
            yb = _gla(pb.reshape(bsz, t, B_PAD_COLS), gw2p, b_gate_b[i], b_norm_g[i])
            wo = w_out_even[i].astype(BF16)
            h = _outproj_ln([ya.reshape(n, A_WIDTH), yb.reshape(n, B_VAL_WIDTH)],
                            [wo[:A_WIDTH], wo[A_WIDTH:]], xf, ln1_g[l], ln1_b[l])
        else:
            (p,) = _proj(xf, [_pad_cols(w_in_odd[i], ODD_PAD_COLS).astype(BF16)])
            pad = lambda z: jnp.pad(z, (0, LANE - z.shape[0])).reshape(1, LANE)
            q, k, v, iq, ika, ikb, iwb = _dsa_prep(p.reshape(bsz, t, ODD_PAD_COLS), _rope_tables(t),
                                                   pad(c_ik_ln_g[i]), pad(c_ik_ln_b[i]))
            blk = lambda z: z.reshape(bsz, t // LANE, LANE, LANE)
            att = _dsa_attn(q, iq, iwb, blk(k), blk(v), blk(ika), blk(ikb), ktop)
            h = _outproj_ln([att.reshape(n, C_WIDTH)], [w_out_odd[i].astype(BF16)], xf,
                            ln1_g[l], ln1_b[l])
        xf = _moe_block(h, rwt, router_bias, exp_w_gate[l], exp_w_up[l], exp_w_down[l],
                        ln2_g[l], ln2_b[l])
    return xf.reshape(bsz, t, d)


def kernel(x, w_in_even, a_mu, a_w0, a_w2, a_a0, a_a2, a_g2, a_kk_scale, a_ka_scale, a_r_k, a_gn_g, a_gn_b, b_gate_w2, b_gate_b, b_norm_g, w_out_even, w_in_odd, c_ik_ln_g, c_ik_ln_b, w_out_odd, ln1_g, ln1_b, ln2_g, ln2_b, router_w, router_bias, exp_w_gate, exp_w_up, exp_w_down):
    return _forward(x, w_in_even, a_mu, a_w0, a_w2, a_a0, a_a2, a_g2, a_kk_scale, a_ka_scale, a_r_k,
                    a_gn_g, a_gn_b, b_gate_w2, b_gate_b, b_norm_g, w_out_even, w_in_odd, c_ik_ln_g,
                    c_ik_ln_b, w_out_odd, ln1_g, ln1_b, ln2_g, ln2_b, router_w, router_bias,
                    exp_w_gate, exp_w_up, exp_w_down)
```

```python
import functools
import math

import numpy as np
import jax
import jax.numpy as jnp
from jax import lax
from jax.experimental import pallas as pl
from jax.experimental.pallas import tpu as pltpu

F32 = jnp.float32
BF16 = jnp.bfloat16
I32 = jnp.int32
I16 = jnp.int16

D_MODEL = 1024
DEPTH = 2
A_HEADS, A_HEAD_DIM = 8, 64
A_WIDTH = A_HEADS * A_HEAD_DIM
A_DECAY_LORA, A_ICLR_LORA, A_GATE_LORA = 64, 64, 128
A_GN_EPS = 64e-5
A_COLS = 3 * A_WIDTH + A_DECAY_LORA + A_ICLR_LORA + A_GATE_LORA
B_HEADS, B_KEY_DIM, B_VAL_DIM = 4, 64, 128
B_KEY_WIDTH = B_HEADS * B_KEY_DIM
B_VAL_WIDTH = B_HEADS * B_VAL_DIM
B_GATE_LORA = 16
B_GATE_TAU = 16.0
B_NORM_EPS = 1e-5
B_COLS = 2 * B_KEY_WIDTH + 2 * B_VAL_WIDTH + B_GATE_LORA
C_HEADS, C_HEAD_DIM = 8, 128
C_WIDTH = C_HEADS * C_HEAD_DIM
C_IDX_HEADS, C_IDX_DIM = 4, 64
C_INDEX_TOPK = 256
ODD_COLS = C_WIDTH + 2 * C_HEAD_DIM + C_IDX_HEADS * C_IDX_DIM + C_IDX_DIM + C_IDX_HEADS
ROPE_THETA = 10000.0
N_EXPERTS, N_GROUPS, TOP_K, D_EXPERT = 16, 4, 2, 256
EXPERTS_PER_GROUP = N_EXPERTS // N_GROUPS
DN_ALPHA = (2 * DEPTH) ** 0.25
LN_EPS = 1e-5

LANE = 128
CHUNK = 64
INV_BLOCK = 16
VMEM_LIMIT = 56 * 1024 * 1024
B_PAD_COLS = 13 * LANE
ODD_PAD_COLS = 13 * LANE
INT_MIN = -2 ** 31
NEG_BIG = -1e30
TIE_NONE = 2 ** 30
HALF = 2 ** 15
KEY_GROUP = 4 * LANE
Q_SCALE = math.log2(math.e) * C_HEAD_DIM ** -0.5


def _cparams(*sem):
    return pltpu.CompilerParams(dimension_semantics=sem, vmem_limit_bytes=VMEM_LIMIT)


def _dot(a, b):
    return jnp.dot(a.astype(BF16), b.astype(BF16), preferred_element_type=F32)


def _dot_nt(a, b):
    return lax.dot_general(a.astype(BF16), b.astype(BF16), (((1,), (1,)), ((), ())),
                           preferred_element_type=F32)


def _dot_tn(a, b):
    return lax.dot_general(a.astype(BF16), b.astype(BF16), (((0,), (0,)), ((), ())),
                           preferred_element_type=F32)


def _split2(a):
    hi = a.astype(BF16)
    lo = (a - hi.astype(F32)).astype(BF16)
    return hi, lo


def _split3(a):
    hi = a.astype(BF16)
    r1 = a - hi.astype(F32)
    mid = r1.astype(BF16)
    lo = (r1 - mid.astype(F32)).astype(BF16)
    return hi, mid, lo


def _dot_exact_lhs(l_bf16, a):
    hi, mid, lo = _split3(a)
    d = lambda z: jnp.dot(l_bf16, z, preferred_element_type=F32)
    return d(hi) + d(mid) + d(lo)


def _dot_exact_rhs(a, r_bf16):
    hi, mid, lo = _split3(a)
    d = lambda z: jnp.dot(z, r_bf16, preferred_element_type=F32)
    return d(hi) + d(mid) + d(lo)


def _dot3(a, b):
    ah, al = _split2(a)
    bh, bl = _split2(b)
    d = lambda x, y: jnp.dot(x, y, preferred_element_type=F32)
    return d(ah, bh) + d(ah, bl) + d(al, bh)


def _dot3_nt(a, b):
    ah, al = _split2(a)
    bh, bl = _split2(b)
    d = lambda x, y: lax.dot_general(x, y, (((1,), (1,)), ((), ())), preferred_element_type=F32)
    return d(ah, bh) + d(ah, bl) + d(al, bh)


def _sigmoid(z):
    return 1.0 / (1.0 + jnp.exp(-z))


def _layer_norm(z, g, b):
    mu = jnp.mean(z, axis=-1, keepdims=True)
    zc = z - mu
    var = jnp.mean(zc * zc, axis=-1, keepdims=True)
    return zc * lax.rsqrt(var + LN_EPS) * g + b


def _proj_kernel(x_ref, *refs):
    n_out = len(refs) // 2
    xb = x_ref[...].astype(BF16)
    for w_ref, o_ref in zip(refs[:n_out], refs[n_out:]):
        o_ref[...] = jnp.dot(xb, w_ref[...], preferred_element_type=F32)


def _proj(x2d, ws, tm=256):
    n, d = x2d.shape
    in_specs = [pl.BlockSpec((tm, d), lambda i: (i, 0))]
    in_specs += [pl.BlockSpec(w.shape, lambda i: (0, 0)) for w in ws]
    out_specs = [pl.BlockSpec((tm, w.shape[1]), lambda i: (i, 0)) for w in ws]
    out_shape = [jax.ShapeDtypeStruct((n, w.shape[1]), F32) for w in ws]
    return pl.pallas_call(
        _proj_kernel, grid=(n // tm,), in_specs=in_specs, out_specs=out_specs,
        out_shape=out_shape, compiler_params=_cparams("parallel"), name="in_proj",
    )(x2d, *ws)


def _rwkv_prep_kernel(p_ref, mu_ref, wwa_ref, w0_ref, a0_ref, g2_ref, kks_ref, kas_ref, hsum_ref,
                      r_ref, lw_ref, k_ref, v_ref, kk_ref, ab_ref, g_ref, carry_ref):
    t = pl.program_id(1)

    @pl.when(t == 0)
    def _():
        carry_ref[...] = jnp.zeros_like(carry_ref)

    p = p_ref[0]
    tm = p.shape[0]
    prev = carry_ref[...]
    row = lax.broadcasted_iota(I32, p.shape, 0)
    shifted = jnp.where(row == 0, prev, pltpu.roll(p, 1, axis=0))
    carry_ref[...] = p[tm - 1:tm, :]
    pm = p + (shifted - p) * mu_ref[...]

    w = A_WIDTH
    r = pm[:, 0:w]
    k = pm[:, w:2 * w]
    v = pm[:, 2 * w:3 * w]
    xwa = pm[:, 3 * w:3 * w + LANE]
    xg = pm[:, 3 * w + LANE:3 * w + 2 * LANE]

    lane = lax.broadcasted_iota(I32, xwa.shape, 1)
    lhs = jnp.where(lane < A_DECAY_LORA, jnp.tanh(xwa), xwa)
    z = _dot3(lhs, wwa_ref[...])
    lw = -_sigmoid(z[:, 0:w] + w0_ref[...]) * math.exp(-0.5)
    a = _sigmoid(z[:, w:2 * w] + a0_ref[...])
    g = _dot(_sigmoid(xg), g2_ref[...])

    kks = k * kks_ref[...]
    ss = _dot_exact_rhs(kks * kks, hsum_ref[...])
    kkn = kks * lax.rsqrt(jnp.maximum(ss, 1e-24))
    kh = k * (1.0 + (a - 1.0) * kas_ref[...])

    r_ref[0] = r
    lw_ref[0] = lw
    k_ref[0] = kh
    v_ref[0] = v
    kk_ref[0] = kkn
    ab_ref[0] = kkn * a
    g_ref[0] = g


def _rwkv_prep(pa, mu, wwa, w0, a0, g2, kks, kas, hsum, tm=256):
    b, t, _ = pa.shape
    w = A_WIDTH
    row = lambda z: z.reshape(1, -1)
    full = lambda z: pl.BlockSpec(z.shape, lambda i, j: (0,) * z.ndim)
    args = [pa, row(mu), wwa, row(w0), row(a0), g2, row(kks), row(kas), hsum]
    in_specs = [pl.BlockSpec((1, tm, A_COLS), lambda i, j: (i, j, 0))] + [full(z) for z in args[1:]]
    out_spec = pl.BlockSpec((1, tm, w), lambda i, j: (i, j, 0))
    return pl.pallas_call(
        _rwkv_prep_kernel, grid=(b, t // tm), in_specs=in_specs, out_specs=[out_spec] * 7,
        out_shape=[jax.ShapeDtypeStruct((b, t, w), F32)] * 7,
        scratch_shapes=[pltpu.VMEM((1, A_COLS), F32)],
        compiler_params=_cparams("parallel", "arbitrary"), name="rwkv_prep",
    )(*args)


def _tri_inv_many(mats, blk_mask, eye_f):
    assert mats[0].shape[0] // INV_BLOCK == 4
    dg = [jnp.where(blk_mask, a, 0.0) for a in mats]
    e = [a - d for a, d in zip(mats, dg)]
    pw = [-d for d in dg]
    dinv = [eye_f + p for p in pw]
    for _ in range(int(math.log2(INV_BLOCK)) - 1):
        pw = [_dot(p, p) for p in pw]
        dinv = [_dot(d, eye_f + p) for d, p in zip(dinv, pw)]
    f = [_dot(d, e_) for d, e_ in zip(dinv, e)]
    m = [d - _dot(f_, d) for d, f_ in zip(dinv, f)]
    f2 = [_dot(f_, f_) for f_ in f]
    return [m_ + _dot(f2_, m_) for m_, f2_ in zip(m, f2)]


def _rwkv_core_kernel(r_ref, lw_ref, k_ref, v_ref, kk_ref, ab_ref, g_ref, rk_ref, gng_ref, gnb_ref,
                      o_ref, s_ref):
    t = pl.program_id(1)

    @pl.when(t == 0)
    def _():
        s_ref[...] = jnp.zeros_like(s_ref)

    c = CHUNK
    hd = A_HEAD_DIM
    n_chunk = r_ref.shape[1] // c
    ri = lax.broadcasted_iota(I32, (c, c), 0)
    ci = lax.broadcasted_iota(I32, (c, c), 1)
    tri_incl = jnp.where(ci <= ri, 1.0, 0.0).astype(BF16)
    eye_f = jnp.where(ci == ri, 1.0, 0.0).astype(F32)
    blk_mask = (ri // INV_BLOCK) == (ci // INV_BLOCK)
    gi = lax.broadcasted_iota(I32, (2 * c, 2 * c), 0)
    gj = lax.broadcasted_iota(I32, (2 * c, 2 * c), 1)
    g_mask = (gj & (c - 1)) < (gi & (c - 1)) + jnp.where(gi >= c, 1, 0)

    def chunk_body(ic, carry):
        sl = pl.ds(pl.multiple_of(ic * c, c), c)
        r = r_ref[0, sl, :]
        lw = lw_ref[0, sl, :]
        k = k_ref[0, sl, :]
        v = v_ref[0, sl, :]
        kk = kk_ref[0, sl, :]
        ab = ab_ref[0, sl, :]
        g = g_ref[0, sl, :]
        cum = _dot_exact_lhs(tri_incl, lw)
        cl = cum[c - 1:c, :]
        e_neg = jnp.exp(-cum)
        e_end = jnp.exp(cl - cum)
        rq = r * jnp.exp(cum)
        kq = kk * jnp.exp(cum - lw)
        bd = ab * e_neg
        kd = k * e_neg
        be = ab * e_end
        ke = k * e_end
        p_end = jnp.exp(cl)
        rkk = r * k * rk_ref[...]

        heads = range(A_HEADS)
        hsl = [slice(h * hd, (h + 1) * hd) for h in heads]
        kq_h = [kq[:, s] for s in hsl]
        rq_h = [rq[:, s] for s in hsl]
        v_hs = [v[:, s] for s in hsl]
        gm = [jnp.where(g_mask,
                        _dot_nt(jnp.concatenate([kq_h[h], rq_h[h]], axis=0),
                                jnp.concatenate([bd[:, hsl[h]], kd[:, hsl[h]]], axis=0)), 0.0)
              for h in heads]
        a_ak = [z[:c, c:] for z in gm]
        a_rb = [z[c:, :c] for z in gm]
        a_rk = [z[c:, c:] for z in gm]
        tinv = _tri_inv_many([z[:c, :c] for z in gm], blk_mask, eye_f)
        wt = [-_dot(t_, k_) for t_, k_ in zip(tinv, kq_h)]
        av = [_dot(a_, v_) for a_, v_ in zip(a_ak, v_hs)]
        ut = [-_dot(t_, a_) for t_, a_ in zip(tinv, av)]
        y0 = [_dot(a_, v_) for a_, v_ in zip(a_rk, v_hs)]
        s0s = [s_ref[h] for h in heads]
        us = [_dot_nt(w_, s_) + u_ for w_, s_, u_ in zip(wt, s0s, ut)]
        y1 = [_dot_nt(r_, s_) for r_, s_ in zip(rq_h, s0s)]
        y2 = [_dot(a_, u_) for a_, u_ in zip(a_rb, us)]
        sn = [_dot_tn(jnp.concatenate([us[h], v_hs[h]], axis=0),
                      jnp.concatenate([be[:, hsl[h]], ke[:, hsl[h]]], axis=0)) for h in heads]
        for h in heads:
            hs = hsl[h]
            v_h = v_hs[h]
            s_ref[h] = s0s[h] * p_end[:, hs] + sn[h]
            y = y1[h] + y2[h] + y0[h]
            ym = jnp.mean(y, axis=-1, keepdims=True)
            yc = y - ym
            yv = jnp.mean(yc * yc, axis=-1, keepdims=True)
            yn = yc * lax.rsqrt(yv + A_GN_EPS) * gng_ref[:, hs] + gnb_ref[:, hs]
            bonus = jnp.sum(rkk[:, hs], axis=-1, keepdims=True) * v_h
            o_ref[0, sl, hs] = (yn + bonus) * g[:, hs]
        return carry

    lax.fori_loop(0, n_chunk, chunk_body, 0)


def _rwkv_core(r, lw, k, v, kk, ab, g, r_k, gn_g, gn_b, tb=256):
    b, t, w = r.shape
    row = lambda z: z.reshape(1, -1)
    blk = pl.BlockSpec((1, tb, w), lambda i, j: (i, j, 0))
    par = pl.BlockSpec((1, w), lambda i, j: (0, 0))
    return pl.pallas_call(
        _rwkv_core_kernel, grid=(b, t // tb), in_specs=[blk] * 7 + [par] * 3, out_specs=blk,
        out_shape=jax.ShapeDtypeStruct((b, t, w), F32),
        scratch_shapes=[pltpu.VMEM((A_HEADS, A_HEAD_DIM, A_HEAD_DIM), F32)],
        compiler_params=_cparams("parallel", "arbitrary"), name="rwkv_core",
    )(r, lw, k, v, kk, ab, g, row(r_k), row(gn_g), row(gn_b))


def _gla_kernel(p_ref, gw2_ref, gb_ref, ng_ref, o_ref, s_ref):
    t = pl.program_id(1)

    @pl.when(t == 0)
    def _():
        s_ref[...] = jnp.zeros_like(s_ref)

    c = CHUNK
    kw, vw = B_KEY_WIDTH, B_VAL_WIDTH
    n_chunk = p_ref.shape[1] // c
    ri = lax.broadcasted_iota(I32, (c, c), 0)
    ci = lax.broadcasted_iota(I32, (c, c), 1)
    causal = ci <= ri
    tri_incl = jnp.where(causal, 1.0, 0.0).astype(BF16)

    def chunk_body(ic, carry):
        sl = pl.ds(pl.multiple_of(ic * c, c), c)
        q = p_ref[0, sl, 0:kw] * (B_KEY_DIM ** -0.5)
        k = p_ref[0, sl, kw:2 * kw]
        v = p_ref[0, sl, 2 * kw:2 * kw + vw]
        g = p_ref[0, sl, 2 * kw + vw:2 * kw + 2 * vw]
        xa = p_ref[0, sl, 2 * kw + 2 * vw:2 * kw + 2 * vw + LANE]
        z = _dot3(xa, gw2_ref[...]) + gb_ref[...]
        log_a = (jnp.minimum(z, 0.0) - jnp.log(1.0 + jnp.exp(-jnp.abs(z)))) * (1.0 / B_GATE_TAU)
        bc = _dot_exact_lhs(tri_incl, log_a)
        bl = bc[c - 1:c, :]
        q_dec = q * jnp.exp(bc)
        k_inv = k * jnp.exp(-bc)
        k_end = k * jnp.exp(bl - bc)
        p_end = jnp.exp(bl)
        heads = range(B_HEADS)
        ksl = [slice(h * B_KEY_DIM, (h + 1) * B_KEY_DIM) for h in heads]
        vsl = [slice(h * B_VAL_DIM, (h + 1) * B_VAL_DIM) for h in heads]
        qd = [q_dec[:, s] for s in ksl]
        v_hs = [v[:, s] for s in vsl]
        att = [jnp.where(causal, _dot_nt(qd[h], k_inv[:, ksl[h]]), 0.0) for h in heads]
        s0s = [s_ref[h] for h in heads]
        o_inter = [_dot_nt(qd[h], s0s[h]) for h in heads]
        o_intra = [_dot(att[h], v_hs[h]) for h in heads]
        sn = [_dot_tn(v_hs[h], k_end[:, ksl[h]]) for h in heads]
        for h in heads:
            s_ref[h] = s0s[h] * p_end[:, ksl[h]] + sn[h]
            o = o_intra[h] + o_inter[h]
            o = o * lax.rsqrt(jnp.mean(o * o, axis=-1, keepdims=True) + B_NORM_EPS)
            g_h = g[:, vsl[h]]
            o_ref[0, sl, vsl[h]] = o * ng_ref[:, vsl[h]] * (g_h * _sigmoid(g_h))
        return carry

    lax.fori_loop(0, n_chunk, chunk_body, 0)


def _gla(pb, gw2p, gate_b, norm_g, tb=256):
    b, t, cols = pb.shape
    return pl.pallas_call(
        _gla_kernel, grid=(b, t // tb),
        in_specs=[pl.BlockSpec((1, tb, cols), lambda i, j: (i, j, 0)),
                  pl.BlockSpec(gw2p.shape, lambda i, j: (0, 0)),
                  pl.BlockSpec((1, B_KEY_WIDTH), lambda i, j: (0, 0)),
                  pl.BlockSpec((1, B_VAL_WIDTH), lambda i, j: (0, 0))],
        out_specs=pl.BlockSpec((1, tb, B_VAL_WIDTH), lambda i, j: (i, j, 0)),
        out_shape=jax.ShapeDtypeStruct((b, t, B_VAL_WIDTH), F32),
        scratch_shapes=[pltpu.VMEM((B_HEADS, B_VAL_DIM, B_KEY_DIM), F32)],
        compiler_params=_cparams("parallel", "arbitrary"), name="gla",
    )(pb, gw2p, gate_b.reshape(1, -1), norm_g.reshape(1, -1))


def _outproj_ln_kernel(*refs):
    n_in = (len(refs) - 4) // 2
    x_ref, g_ref, b_ref, o_ref = refs[2 * n_in:]
    mix = None
    for y_ref, w_ref in zip(refs[:n_in], refs[n_in:2 * n_in]):
        d = jnp.dot(y_ref[...].astype(BF16), w_ref[...], preferred_element_type=F32)
        mix = d if mix is None else mix + d
    o_ref[...] = _layer_norm(DN_ALPHA * x_ref[...] + mix, g_ref[...], b_ref[...])


def _outproj_ln(ys, ws, x2d, g, b, tm=256):
    n, d = x2d.shape
    in_specs = [pl.BlockSpec((tm, y.shape[1]), lambda i: (i, 0)) for y in ys]
    in_specs += [pl.BlockSpec(w.shape, lambda i: (0, 0)) for w in ws]
    in_specs += [pl.BlockSpec((tm, d), lambda i: (i, 0)),
                 pl.BlockSpec((1, d), lambda i: (0, 0)), pl.BlockSpec((1, d), lambda i: (0, 0))]
    return pl.pallas_call(
        _outproj_ln_kernel, grid=(n // tm,), in_specs=in_specs,
        out_specs=pl.BlockSpec((tm, d), lambda i: (i, 0)),
        out_shape=jax.ShapeDtypeStruct((n, d), F32),
        compiler_params=_cparams("parallel"), name="out_proj_ln",
    )(*ys, *ws, x2d, g.reshape(1, -1), b.reshape(1, -1))


def _router_kernel(h_ref, rwt_ref, rb_ref, o_ref):
    logits = _dot3_nt(rwt_ref[...], h_ref[...])
    s = _sigmoid(logits)
    sel = s + rb_ref[...]
    s_rows = [s[e:e + 1, :] for e in range(N_EXPERTS)]
    rows = [sel[e:e + 1, :] for e in range(N_EXPERTS)]
    best_val, best = None, None
    for gidx in range(N_GROUPS):
        mem = rows[gidx * EXPERTS_PER_GROUP:(gidx + 1) * EXPERTS_PER_GROUP]
        gs = None
        for i in range(EXPERTS_PER_GROUP):
            for j in range(i + 1, EXPERTS_PER_GROUP):
                pair = mem[i] + mem[j]
                gs = pair if gs is None else jnp.maximum(gs, pair)
        if best_val is None:
            best_val, best = gs, jnp.zeros(gs.shape, I32)
        else:
            upd = gs > best_val
            best = jnp.where(upd, gidx, best)
            best_val = jnp.where(upd, gs, best_val)
    vals = [jnp.where(best == (e // EXPERTS_PER_GROUP), rows[e], -jnp.inf) for e in range(N_EXPERTS)]

    def arg_top(vs):
        m = functools.reduce(jnp.maximum, vs)
        idx = jnp.full(m.shape, N_EXPERTS, I32)
        for e in reversed(range(N_EXPERTS)):
            idx = jnp.where(vs[e] == m, e, idx)
        return idx

    i1 = arg_top(vals)
    i2 = arg_top([jnp.where(i1 == e, -jnp.inf, vals[e]) for e in range(N_EXPERTS)])
    g1 = functools.reduce(jnp.add, [jnp.where(i1 == e, s_rows[e], 0.0) for e in range(N_EXPERTS)])
    g2 = functools.reduce(jnp.add, [jnp.where(i2 == e, s_rows[e], 0.0) for e in range(N_EXPERTS)])
    tot = g1 + g2
    for e in range(N_EXPERTS):
        o_ref[e:e + 1, :] = jnp.where(i1 == e, g1 / tot, 0.0) + jnp.where(i2 == e, g2 / tot, 0.0)


def _router(h2d, rwt, rbias, tm=256):
    n, d = h2d.shape
    return pl.pallas_call(
        _router_kernel, grid=(n // tm,),
        in_specs=[pl.BlockSpec((tm, d), lambda i: (i, 0)),
                  pl.BlockSpec((N_EXPERTS, d), lambda i: (0, 0)),
                  pl.BlockSpec((N_EXPERTS, 1), lambda i: (0, 0))],
        out_specs=pl.BlockSpec((N_EXPERTS, tm), lambda i: (0, i)),
        out_shape=jax.ShapeDtypeStruct((N_EXPERTS, n), F32),
        compiler_params=_cparams("parallel"), name="router",
    )(h2d, rwt, rbias.reshape(-1, 1))


def _moe_kernel(h_ref, comb_ref, wgu_ref, wd_ref, g_ref, b_ref, o_ref, acc_ref, hb_ref):
    e = pl.program_id(1)

    @pl.when(e == 0)
    def _():
        acc_ref[...] = jnp.zeros_like(acc_ref)
        hb_ref[...] = h_ref[...].astype(BF16)

    gu = jnp.dot(hb_ref[...], wgu_ref[0], preferred_element_type=F32)
    gt, up = gu[:, :D_EXPERT], gu[:, D_EXPERT:]
    comb = comb_ref[...]
    lane = lax.broadcasted_iota(I32, comb.shape, 1)
    ce = jnp.sum(jnp.where(lane == e, comb, 0.0), axis=-1, keepdims=True)
    act = (gt * _sigmoid(gt)) * up * ce
    acc_ref[...] += jnp.dot(act.astype(BF16), wd_ref[0], preferred_element_type=F32)

    @pl.when(e == N_EXPERTS - 1)
    def _():
        o_ref[...] = _layer_norm(DN_ALPHA * h_ref[...] + acc_ref[...], g_ref[...], b_ref[...])


def _moe(h2d, comb, wgu, wd, g, b, tm=1024):
    n, d = h2d.shape
    tm = min(tm, n)
    return pl.pallas_call(
        _moe_kernel, grid=(n // tm, N_EXPERTS),
        in_specs=[pl.BlockSpec((tm, d), lambda i, e: (i, 0)),
                  pl.BlockSpec((tm, N_EXPERTS), lambda i, e: (i, 0)),
                  pl.BlockSpec((1, d, 2 * D_EXPERT), lambda i, e: (e, 0, 0)),
                  pl.BlockSpec((1, D_EXPERT, d), lambda i, e: (e, 0, 0)),
                  pl.BlockSpec((1, d), lambda i, e: (0, 0)),
                  pl.BlockSpec((1, d), lambda i, e: (0, 0))],
        out_specs=pl.BlockSpec((tm, d), lambda i, e: (i, 0)),
        out_shape=jax.ShapeDtypeStruct((n, d), F32),
        scratch_shapes=[pltpu.VMEM((tm, d), F32), pltpu.VMEM((tm, d), BF16)],
        compiler_params=_cparams("parallel", "arbitrary"), name="moe",
    )(h2d, comb, wgu, wd, g.reshape(1, -1), b.reshape(1, -1))


def _rope_full(z, cos, sin_signed):
    return z * cos + pltpu.roll(z, LANE // 2, axis=1) * sin_signed


def _rope_half(z, cos, sin_signed, first_half):
    partner = jnp.where(first_half, pltpu.roll(z, LANE - C_IDX_DIM // 2, axis=1),
                        pltpu.roll(z, C_IDX_DIM // 2, axis=1))
    return z * cos + partner * sin_signed


def _dsa_prep_kernel(p_ref, c128_ref, s128_ref, c64_ref, s64_ref, lng_ref, lnb_ref,
                     q_ref, k_ref, v_ref, iq_ref, ika_ref, ikb_ref, iw_ref):
    c128, s128 = c128_ref[...], s128_ref[...]
    c64, s64 = c64_ref[...], s64_ref[...]
    lane = lax.broadcasted_iota(I32, c64.shape, 1)
    first_half = (lane & (C_IDX_DIM - 1)) < C_IDX_DIM // 2
    for h in range(C_HEADS):
        hs = slice(h * LANE, (h + 1) * LANE)
        q_ref[0, :, hs] = (_rope_full(p_ref[0, :, hs], c128, s128) * Q_SCALE).astype(BF16)
    k0 = C_WIDTH
    k_ref[0] = _rope_full(p_ref[0, :, k0:k0 + LANE], c128, s128).astype(BF16)
    v_ref[0] = p_ref[0, :, k0 + LANE:k0 + 2 * LANE].astype(BF16)
    i0 = k0 + 2 * LANE
    for j in range(C_IDX_HEADS * C_IDX_DIM // LANE):
        js = slice(j * LANE, (j + 1) * LANE)
        z = p_ref[0, :, i0 + j * LANE:i0 + (j + 1) * LANE]
        iq_ref[0, :, js] = (_rope_half(z, c64, s64, first_half) * C_IDX_DIM ** -0.5).astype(BF16)
    t0 = i0 + C_IDX_HEADS * C_IDX_DIM
    tile = p_ref[0, :, t0:t0 + LANE]
    is_key = lane < C_IDX_DIM
    mu = jnp.sum(jnp.where(is_key, tile, 0.0), axis=-1, keepdims=True) * (1.0 / C_IDX_DIM)
    zc = jnp.where(is_key, tile - mu, 0.0)
    var = jnp.sum(zc * zc, axis=-1, keepdims=True) * (1.0 / C_IDX_DIM)
    ikn = zc * lax.rsqrt(var + LN_EPS) * lng_ref[...] + lnb_ref[...]
    ikr = jnp.where(is_key, _rope_half(ikn, c64, s64, first_half), 0.0)
    ika_ref[0] = ikr.astype(BF16)
    ikb_ref[0] = pltpu.roll(ikr, C_IDX_DIM, axis=1).astype(BF16)
    iw_ref[0] = tile * C_IDX_HEADS ** -0.5


def _dsa_prep(p, tabs, lng, lnb, tm=256):
    b, t, cols = p.shape
    tab = pl.BlockSpec((tm, LANE), lambda i, j: (j, 0))
    par = pl.BlockSpec((1, LANE), lambda i, j: (0, 0))
    o = lambda w: pl.BlockSpec((1, tm, w), lambda i, j: (i, j, 0))
    widths = [C_WIDTH, LANE, LANE, C_IDX_HEADS * C_IDX_DIM, LANE, LANE, LANE]
    dtypes = [BF16, BF16, BF16, BF16, BF16, BF16, F32]
    return pl.pallas_call(
        _dsa_prep_kernel, grid=(b, t // tm),
        in_specs=[pl.BlockSpec((1, tm, cols), lambda i, j: (i, j, 0)), tab, tab, tab, tab, par, par],
        out_specs=[o(w) for w in widths],
        out_shape=[jax.ShapeDtypeStruct((b, t, w), dt) for w, dt in zip(widths, dtypes)],
        compiler_params=_cparams("parallel", "parallel"), name="dsa_prep",
    )(p, *tabs, lng, lnb)


def _fold8(z, op):
    return op(z.reshape(z.shape[0] // 8, 8, z.shape[1]), axis=0)


def _dsa_attn_kernel(q_ref, iq_ref, iw_ref, k_ref, vt_ref, ikab_ref, o_ref,
                     key_ref, hi_ref, lo_ref, tie_ref, bias_ref, s_ref, acc_ref, *, ktop, idx_bits):
    i = pl.program_id(1)
    tq = q_ref.shape[1]
    kg = k_ref.shape[2]
    assert tq == LANE and kg % tq == 0
    ng = i // (kg // tq) + 1
    keypos0 = lax.broadcasted_iota(I32, (kg, tq), 0)
    qpos = i * tq + lax.broadcasted_iota(I32, (kg, tq), 1)

    iq = iq_ref[0]
    rhs_i = jnp.concatenate([iq[:, :LANE], iq[:, LANE:]], axis=0)
    w = [iw_ref[0, h:h + 1, :] for h in range(C_IDX_HEADS)]

    def score_body(g, carry):
        sab = _dot_nt(ikab_ref[0, g], rhs_i)
        relu = lambda z: jnp.maximum(z, 0.0)
        sc = (relu(sab[:kg, :tq]) * w[0] + relu(sab[kg:, :tq]) * w[1]
              + relu(sab[:kg, tq:]) * w[2] + relu(sab[kg:, tq:]) * w[3]) + 0.0
        bits = pltpu.bitcast(sc, I32)
        skey = bits ^ ((bits >> 31) & 0x7FFFFFFF)
        skey = jnp.where(g * kg + keypos0 <= qpos, skey, INT_MIN)
        key_ref[g] = skey
        hi_ref[g] = (skey >> 16).astype(I16)
        lo_ref[g] = ((skey & 0xFFFF) - HALF).astype(I16)
        return carry

    lax.fori_loop(0, ng, score_body, 0)

    def count16(ref, cand):
        c16 = cand.astype(I16)

        def body(g, acc):
            ind = jnp.where(ref[g] >= c16, jnp.int16(1), jnp.int16(0))
            parts = ind.reshape(kg // 16, 16, tq)
            vals = [parts[j] for j in range(kg // 16)]
            while len(vals) > 1:
                vals = [vals[j] + vals[j + 1] for j in range(0, len(vals), 2)]
            return acc + vals[0]

        acc = lax.fori_loop(0, ng, body, jnp.zeros((16, tq), I16))
        return jnp.sum(acc.astype(I32), axis=0, keepdims=True)

    def bisect16(ref, c_start):
        def body(it, carry):
            tau, c_lo = carry
            cand = tau + lax.shift_left(jnp.int32(1), 15 - it)
            cnt = count16(ref, cand)
            take = cnt >= ktop
            return jnp.where(take, cand, tau), jnp.where(take, cnt, c_lo)

        return lax.fori_loop(0, 16, body, (jnp.full((1, tq), -HALF, I32), c_start))

    tau_hi, c_hi = bisect16(hi_ref, jnp.zeros((1, tq), I32) + ng * kg)
    th16 = tau_hi.astype(I16)

    def resid_body(g, carry):
        hi = hi_ref[g]
        lo_ref[g] = jnp.where(hi == th16, lo_ref[g],
                              jnp.where(hi > th16, jnp.int16(HALF - 1), jnp.int16(-HALF)))
        return carry

    lax.fori_loop(0, ng, resid_body, 0)
    tau_lo, c_lo = bisect16(lo_ref, c_hi)
    tau = lax.shift_left(tau_hi, 16) + (tau_lo + HALF)

    def count(ref, pred):
        def body(g, acc):
            return acc + _fold8(jnp.where(pred(ref[g]), 1, 0), jnp.sum)
        acc = lax.fori_loop(0, ng, body, jnp.zeros((8, tq), I32))
        return jnp.sum(acc, axis=0, keepdims=True)

    tie_rows = jnp.where(c_lo > ktop, jnp.where(tau > INT_MIN, 1, 0), 0)
    need_tie = jnp.max(tie_rows)

    @pl.when(need_tie == 0)
    def _():
        thr = jnp.maximum(tau, INT_MIN + 1)

        def body(g, carry):
            bias_ref[g] = jnp.where(key_ref[g] >= thr, 0.0, NEG_BIG)
            return carry

        lax.fori_loop(0, ng, body, 0)

    @pl.when(need_tie > 0)
    def _():
        need = ktop - count(key_ref, lambda x: x > tau)

        def tie_body(g, carry):
            key = key_ref[g]
            idx = jnp.where(key == INT_MIN, TIE_NONE, g * kg + keypos0)
            tie_ref[g] = jnp.where(key == tau, idx, TIE_NONE)
            return carry

        lax.fori_loop(0, ng, tie_body, 0)

        def bisect_idx(it, ans):
            cand = ans + lax.shift_left(jnp.int32(1), idx_bits - 1 - it)
            cnt = count(tie_ref, lambda x: x < cand)
            return jnp.where(cnt < need, cand, ans)

        jstar = lax.fori_loop(0, idx_bits, bisect_idx, jnp.zeros((1, tq), I32))

        def body(g, carry):
            sel = jnp.where(key_ref[g] > tau, 1, jnp.where(tie_ref[g] <= jstar, 1, 0))
            bias_ref[g] = jnp.where(sel > 0, 0.0, NEG_BIG)
            return carry

        lax.fori_loop(0, ng, body, 0)

    q = q_ref[0]
    pairs = range(C_HEADS // 2)
    q_rows = [jnp.concatenate([q[:, (2 * j) * LANE:(2 * j + 1) * LANE],
                               q[:, (2 * j + 1) * LANE:(2 * j + 2) * LANE]], axis=0) for j in pairs]

    def pass1(g, m8s):
        k_g = k_ref[0, g]
        bias = bias_ref[g]
        out = []
        for j in pairs:
            s2 = _dot_nt(k_g, q_rows[j])
            for u in range(2):
                s = s2[:, u * tq:(u + 1) * tq] + bias
                s_ref[2 * j + u, g] = s
                out.append(jnp.maximum(m8s[2 * j + u], _fold8(s, jnp.max)))
        return tuple(out)

    m8s = lax.fori_loop(0, ng, pass1, tuple(jnp.full((8, tq), NEG_BIG, F32) for _ in range(C_HEADS)))
    ms = [jnp.max(m8, axis=0, keepdims=True) for m8 in m8s]
    acc_ref[...] = jnp.zeros_like(acc_ref)

    def pass2(g, l8s):
        vt_g = vt_ref[0, g]
        out = []
        for j in pairs:
            ps = [jnp.exp2(s_ref[2 * j + u, g] - ms[2 * j + u]) for u in range(2)]
            acc_ref[j] += jnp.dot(vt_g, jnp.concatenate(ps, axis=1).astype(BF16),
                                  preferred_element_type=F32)
            out += [l8s[2 * j + u] + _fold8(ps[u], jnp.sum) for u in range(2)]
        return tuple(out)

    l8s = lax.fori_loop(0, ng, pass2, tuple(jnp.zeros((8, tq), F32) for _ in range(C_HEADS)))
    for h in range(C_HEADS):
        acc = acc_ref[h // 2][:, (h % 2) * tq:(h % 2 + 1) * tq]
        o_ref[0, :, h * LANE:(h + 1) * LANE] = (acc / jnp.sum(l8s[h], axis=0, keepdims=True)).T


def _dsa_attn(q, iq, iwt, kg4, vtg4, ikab4, ktop):
    b, t, _ = q.shape
    tq = LANE
    ngt, kg = kg4.shape[1], kg4.shape[2]
    qspec = lambda w: pl.BlockSpec((1, tq, w), lambda i, j: (i, j, 0))
    whole = lambda z: pl.BlockSpec((1,) + z.shape[1:], lambda i, j: (i, 0, 0, 0))
    kern = functools.partial(_dsa_attn_kernel, ktop=ktop, idx_bits=int(math.log2(t)))
    scratch = [pltpu.VMEM((ngt, kg, tq), I32), pltpu.VMEM((ngt, kg, tq), I16),
               pltpu.VMEM((ngt, kg, tq), I16), pltpu.VMEM((ngt, kg, tq), I32),
               pltpu.VMEM((ngt, kg, tq), F32), pltpu.VMEM((C_HEADS, ngt, kg, tq), F32),
               pltpu.VMEM((C_HEADS // 2, C_HEAD_DIM, 2 * tq), F32)]
    return pl.pallas_call(
        kern, grid=(b, t // tq),
        in_specs=[qspec(C_WIDTH), qspec(C_IDX_HEADS * C_IDX_DIM),
                  pl.BlockSpec((1, C_IDX_HEADS, tq), lambda i, j: (i, 0, j)),
                  whole(kg4), whole(vtg4), whole(ikab4)],
        out_specs=qspec(C_WIDTH),
        out_shape=jax.ShapeDtypeStruct((b, t, C_WIDTH), F32),
        scratch_shapes=scratch,
        compiler_params=_cparams("parallel", "arbitrary"), name="dsa_attn",
    )(q, iq, iwt, kg4, vtg4, ikab4)


def _rope_tables(t):
    pos = jnp.arange(t, dtype=F32)

    def tab(dim):
        half = dim // 2
        inv = ROPE_THETA ** (-jnp.arange(half, dtype=F32) / half)
        ang = pos[:, None] * inv[None, :]
        cos, sin = jnp.cos(ang), jnp.sin(ang)
        reps = LANE // dim
        return (jnp.tile(jnp.concatenate([cos, cos], axis=-1), (1, reps)),
                jnp.tile(jnp.concatenate([-sin, sin], axis=-1), (1, reps)))

    c128, s128 = tab(C_HEAD_DIM)
    c64, s64 = tab(C_IDX_DIM)
    return c128, s128, c64, s64


def _pad_cols(w, cols):
    return jnp.pad(w, ((0, 0), (0, cols - w.shape[1])))


def _moe_block(h, rwt, router_bias, w_gate, w_up, w_down, g, b):
    comb = _router(h, rwt, router_bias).T
    wgu = jnp.concatenate([w_gate, w_up], axis=-1).astype(BF16)
    return _moe(h, comb, wgu, w_down.astype(BF16), g, b)


@jax.jit
def _forward(x, w_in_even, a_mu, a_w0, a_w2, a_a0, a_a2, a_g2, a_kk_scale, a_ka_scale, a_r_k,
             a_gn_g, a_gn_b, b_gate_w2, b_gate_b, b_norm_g, w_out_even, w_in_odd, c_ik_ln_g,
             c_ik_ln_b, w_out_odd, ln1_g, ln1_b, ln2_g, ln2_b, router_w, router_bias,
             exp_w_gate, exp_w_up, exp_w_down):
    bsz, t, d = x.shape
    assert d == D_MODEL and t % 256 == 0 and (t & (t - 1)) == 0
    n = bsz * t
    ktop = min(C_INDEX_TOPK, t // 4)
    rwt = router_w.T
    xf = x.reshape(n, d)
    hsum = (jnp.arange(A_WIDTH)[:, None] // A_HEAD_DIM
            == jnp.arange(A_WIDTH)[None, :] // A_HEAD_DIM).astype(BF16)

    for l in range(DEPTH):
        i = l // 2
        if l % 2 == 0:
            w = w_in_even[i]
            pa, pb = _proj(xf, [w[:, :A_COLS].astype(BF16),
                                _pad_cols(w[:, A_COLS:], B_PAD_COLS).astype(BF16)])
            wwa = jnp.zeros((LANE, 2 * A_WIDTH), F32)
            wwa = wwa.at[:A_DECAY_LORA, :A_WIDTH].set(a_w2[i]).at[A_DECAY_LORA:, A_WIDTH:].set(a_a2[i])
            parts = _rwkv_prep(pa.reshape(bsz, t, A_COLS), a_mu[i], wwa, a_w0[i], a_a0[i], a_g2[i],
                               a_kk_scale[i], a_ka_scale[i], hsum)
            ya = _rwkv_core(*parts, a_r_k[i].reshape(-1), a_gn_g[i], a_gn_b[i])
            gw2p = jnp.zeros((LANE, B_KEY_WIDTH), F32).at[:B_GATE_LORA].set(b_gate_w2[i])
            yb = _gla(pb.reshape(bsz, t, B_PAD_COLS), gw2p, b_gate_b[i], b_norm_g[i])
            wo = w_out_even[i].astype(BF16)
            h = _outproj_ln([ya.reshape(n, A_WIDTH), yb.reshape(n, B_VAL_WIDTH)],
                            [wo[:A_WIDTH], wo[A_WIDTH:]], xf, ln1_g[l], ln1_b[l])
        else:
            (p,) = _proj(xf, [_pad_cols(w_in_odd[i], ODD_PAD_COLS).astype(BF16)])
            pad = lambda z: jnp.pad(z, (0, LANE - z.shape[0])).reshape(1, LANE)
            q, k, v, iq, ika, ikb, iw = _dsa_prep(p.reshape(bsz, t, ODD_PAD_COLS), _rope_tables(t),
                                                  pad(c_ik_ln_g[i]), pad(c_ik_ln_b[i]))
            kgrp = min(KEY_GROUP, t)
            grp = lambda z: z.reshape(bsz, t // kgrp, kgrp, LANE)
            iwt = jnp.swapaxes(iw[:, :, C_IDX_DIM:C_IDX_DIM + C_IDX_HEADS], 1, 2)
            att = _dsa_attn(q, iq, iwt, grp(k), jnp.swapaxes(grp(v), 2, 3),
                            jnp.concatenate([grp(ika), grp(ikb)], axis=2), ktop)
            h = _outproj_ln([att.reshape(n, C_WIDTH)], [w_out_odd[i].astype(BF16)], xf,
                            ln1_g[l], ln1_b[l])
        xf = _moe_block(h, rwt, router_bias, exp_w_gate[l], exp_w_up[l], exp_w_down[l],
                        ln2_g[l], ln2_b[l])
    return xf.reshape(bsz, t, d)


def kernel(x, w_in_even, a_mu, a_w0, a_w2, a_a0, a_a2, a_g2, a_kk_scale, a_ka_scale, a_r_k, a_gn_g, a_gn_b, b_gate_w2, b_gate_b, b_norm_g, w_out_even, w_in_odd, c_ik_ln_g, c_ik_ln_b, w_out_odd, ln1_g, ln1_b, ln2_g, ln2_b, router_w, router_bias, exp_w_gate, exp_w_up, exp_w_down):
    return _forward(x, w_in_even, a_mu, a_w0, a_w2, a_a0, a_a2, a_g2, a_kk_scale, a_ka_scale, a_r_k,
                    a_gn_g, a_gn_b, b_gate_w2, b_gate_b, b_norm_g, w_out_even, w_in_odd, c_ik_ln_g,
                    c_ik_ln_b, w_out_odd, ln1_g, ln1_b, ln2_g, ln2_b, router_w, router_bias,
                    exp_w_gate, exp_w_up, exp_w_down)
```

```python
import functools
import math

import numpy as np
import jax
import jax.numpy as jnp
from jax import lax
from jax.experimental import pallas as pl
from jax.experimental.pallas import tpu as pltpu

F32 = jnp.float32
BF16 = jnp.bfloat16
I32 = jnp.int32

D_MODEL = 1024
DEPTH = 2
A_HEADS, A_HEAD_DIM = 8, 64
A_WIDTH = A_HEADS * A_HEAD_DIM
A_DECAY_LORA, A_ICLR_LORA, A_GATE_LORA = 64, 64, 128
A_GN_EPS = 64e-5
A_COLS = 3 * A_WIDTH + A_DECAY_LORA + A_ICLR_LORA + A_GATE_LORA
B_HEADS, B_KEY_DIM, B_VAL_DIM = 4, 64, 128
B_KEY_WIDTH = B_HEADS * B_KEY_DIM
B_VAL_WIDTH = B_HEADS * B_VAL_DIM
B_GATE_LORA = 16
B_GATE_TAU = 16.0
B_NORM_EPS = 1e-5
B_COLS = 2 * B_KEY_WIDTH + 2 * B_VAL_WIDTH + B_GATE_LORA
C_HEADS, C_HEAD_DIM = 8, 128
C_WIDTH = C_HEADS * C_HEAD_DIM
C_IDX_HEADS, C_IDX_DIM = 4, 64
C_INDEX_TOPK = 256
ODD_COLS = C_WIDTH + 2 * C_HEAD_DIM + C_IDX_HEADS * C_IDX_DIM + C_IDX_DIM + C_IDX_HEADS
ROPE_THETA = 10000.0
N_EXPERTS, N_GROUPS, TOP_K, D_EXPERT = 16, 4, 2, 256
EXPERTS_PER_GROUP = N_EXPERTS // N_GROUPS
DN_ALPHA = (2 * DEPTH) ** 0.25
LN_EPS = 1e-5

LANE = 128
CHUNK = 64
INV_BLOCK = 16
VMEM_LIMIT = 56 * 1024 * 1024
B_PAD_COLS = 13 * LANE
ODD_PAD_COLS = 13 * LANE
INT_MIN = -2 ** 31
NEG_BIG = -1e30
TIE_NONE = 2 ** 30
F32_LOWEST = float(np.finfo(np.float32).min)
KEY_NEG_INF = INT_MIN + 0x7FFFFF
KEY_GROUP = 4 * LANE
Q_SCALE = math.log2(math.e) * C_HEAD_DIM ** -0.5


def _cparams(*sem):
    return pltpu.CompilerParams(dimension_semantics=sem, vmem_limit_bytes=VMEM_LIMIT)


def _dot(a, b):
    return jnp.dot(a.astype(BF16), b.astype(BF16), preferred_element_type=F32)


def _dot_nt(a, b):
    return lax.dot_general(a.astype(BF16), b.astype(BF16), (((1,), (1,)), ((), ())),
                           preferred_element_type=F32)


def _dot_tn(a, b):
    return lax.dot_general(a.astype(BF16), b.astype(BF16), (((0,), (0,)), ((), ())),
                           preferred_element_type=F32)


def _split2(a):
    hi = a.astype(BF16)
    lo = (a - hi.astype(F32)).astype(BF16)
    return hi, lo


def _split3(a):
    hi = a.astype(BF16)
    r1 = a - hi.astype(F32)
    mid = r1.astype(BF16)
    lo = (r1 - mid.astype(F32)).astype(BF16)
    return hi, mid, lo


def _dot_exact_lhs(l_bf16, a):
    hi, mid, lo = _split3(a)
    d = lambda z: jnp.dot(l_bf16, z, preferred_element_type=F32)
    return d(hi) + d(mid) + d(lo)


def _dot_exact_rhs(a, r_bf16):
    hi, mid, lo = _split3(a)
    d = lambda z: jnp.dot(z, r_bf16, preferred_element_type=F32)
    return d(hi) + d(mid) + d(lo)


def _dot3(a, b):
    ah, al = _split2(a)
    bh, bl = _split2(b)
    d = lambda x, y: jnp.dot(x, y, preferred_element_type=F32)
    return d(ah, bh) + d(ah, bl) + d(al, bh)


def _dot3_nt(a, b):
    ah, al = _split2(a)
    bh, bl = _split2(b)
    d = lambda x, y: lax.dot_general(x, y, (((1,), (1,)), ((), ())), preferred_element_type=F32)
    return d(ah, bh) + d(ah, bl) + d(al, bh)


def _sigmoid(z):
    return 1.0 / (1.0 + jnp.exp(-z))


def _layer_norm(z, g, b):
    mu = jnp.mean(z, axis=-1, keepdims=True)
    zc = z - mu
    var = jnp.mean(zc * zc, axis=-1, keepdims=True)
    return zc * lax.rsqrt(var + LN_EPS) * g + b


def _proj_kernel(x_ref, *refs):
    n_out = len(refs) // 2
    xb = x_ref[...].astype(BF16)
    for w_ref, o_ref in zip(refs[:n_out], refs[n_out:]):
        o_ref[...] = jnp.dot(xb, w_ref[...], preferred_element_type=F32)


def _proj(x2d, ws, tm=256):
    n, d = x2d.shape
    in_specs = [pl.BlockSpec((tm, d), lambda i: (i, 0))]
    in_specs += [pl.BlockSpec(w.shape, lambda i: (0, 0)) for w in ws]
    out_specs = [pl.BlockSpec((tm, w.shape[1]), lambda i: (i, 0)) for w in ws]
    out_shape = [jax.ShapeDtypeStruct((n, w.shape[1]), F32) for w in ws]
    return pl.pallas_call(
        _proj_kernel, grid=(n // tm,), in_specs=in_specs, out_specs=out_specs,
        out_shape=out_shape, compiler_params=_cparams("parallel"), name="in_proj",
    )(x2d, *ws)


def _rwkv_prep_kernel(p_ref, mu_ref, wwa_ref, w0_ref, a0_ref, g2_ref, kks_ref, kas_ref, hsum_ref,
                      r_ref, lw_ref, k_ref, v_ref, kk_ref, ab_ref, g_ref, carry_ref):
    t = pl.program_id(1)

    @pl.when(t == 0)
    def _():
        carry_ref[...] = jnp.zeros_like(carry_ref)

    p = p_ref[0]
    tm = p.shape[0]
    prev = carry_ref[...]
    row = lax.broadcasted_iota(I32, p.shape, 0)
    shifted = jnp.where(row == 0, prev, pltpu.roll(p, 1, axis=0))
    carry_ref[...] = p[tm - 1:tm, :]
    pm = p + (shifted - p) * mu_ref[...]

    w = A_WIDTH
    r = pm[:, 0:w]
    k = pm[:, w:2 * w]
    v = pm[:, 2 * w:3 * w]
    xwa = pm[:, 3 * w:3 * w + LANE]
    xg = pm[:, 3 * w + LANE:3 * w + 2 * LANE]

    lane = lax.broadcasted_iota(I32, xwa.shape, 1)
    lhs = jnp.where(lane < A_DECAY_LORA, jnp.tanh(xwa), xwa)
    z = _dot3(lhs, wwa_ref[...])
    lw = -_sigmoid(z[:, 0:w] + w0_ref[...]) * math.exp(-0.5)
    a = _sigmoid(z[:, w:2 * w] + a0_ref[...])
    g = _dot(_sigmoid(xg), g2_ref[...])

    kks = k * kks_ref[...]
    ss = _dot_exact_rhs(kks * kks, hsum_ref[...])
    kkn = kks * lax.rsqrt(jnp.maximum(ss, 1e-24))
    kh = k * (1.0 + (a - 1.0) * kas_ref[...])

    r_ref[0] = r
    lw_ref[0] = lw
    k_ref[0] = kh
    v_ref[0] = v
    kk_ref[0] = kkn
    ab_ref[0] = kkn * a
    g_ref[0] = g


def _rwkv_prep(pa, mu, wwa, w0, a0, g2, kks, kas, hsum, tm=256):
    b, t, _ = pa.shape
    w = A_WIDTH
    row = lambda z: z.reshape(1, -1)
    full = lambda z: pl.BlockSpec(z.shape, lambda i, j: (0,) * z.ndim)
    args = [pa, row(mu), wwa, row(w0), row(a0), g2, row(kks), row(kas), hsum]
    in_specs = [pl.BlockSpec((1, tm, A_COLS), lambda i, j: (i, j, 0))] + [full(z) for z in args[1:]]
    out_spec = pl.BlockSpec((1, tm, w), lambda i, j: (i, j, 0))
    return pl.pallas_call(
        _rwkv_prep_kernel, grid=(b, t // tm), in_specs=in_specs, out_specs=[out_spec] * 7,
        out_shape=[jax.ShapeDtypeStruct((b, t, w), F32)] * 7,
        scratch_shapes=[pltpu.VMEM((1, A_COLS), F32)],
        compiler_params=_cparams("parallel", "arbitrary"), name="rwkv_prep",
    )(*args)


def _tri_inv_many(mats, blk_mask, eye_f):
    assert mats[0].shape[0] // INV_BLOCK == 4
    dg = [jnp.where(blk_mask, a, 0.0) for a in mats]
    e = [a - d for a, d in zip(mats, dg)]
    pw = [-d for d in dg]
    dinv = [eye_f + p for p in pw]
    for _ in range(int(math.log2(INV_BLOCK)) - 1):
        pw = [_dot(p, p) for p in pw]
        dinv = [_dot(d, eye_f + p) for d, p in zip(dinv, pw)]
    f = [_dot(d, e_) for d, e_ in zip(dinv, e)]
    m = [d - _dot(f_, d) for d, f_ in zip(dinv, f)]
    f2 = [_dot(f_, f_) for f_ in f]
    return [m_ + _dot(f2_, m_) for m_, f2_ in zip(m, f2)]


def _rwkv_core_kernel(r_ref, lw_ref, k_ref, v_ref, kk_ref, ab_ref, g_ref, rk_ref, gng_ref, gnb_ref,
                      o_ref, s_ref):
    t = pl.program_id(1)

    @pl.when(t == 0)
    def _():
        s_ref[...] = jnp.zeros_like(s_ref)

    c = CHUNK
    hd = A_HEAD_DIM
    n_chunk = r_ref.shape[1] // c
    ri = lax.broadcasted_iota(I32, (c, c), 0)
    ci = lax.broadcasted_iota(I32, (c, c), 1)
    tri_incl = jnp.where(ci <= ri, 1.0, 0.0).astype(BF16)
    eye_f = jnp.where(ci == ri, 1.0, 0.0).astype(F32)
    blk_mask = (ri // INV_BLOCK) == (ci // INV_BLOCK)
    gi = lax.broadcasted_iota(I32, (2 * c, 2 * c), 0)
    gj = lax.broadcasted_iota(I32, (2 * c, 2 * c), 1)
    g_mask = (gj & (c - 1)) < (gi & (c - 1)) + jnp.where(gi >= c, 1, 0)

    def chunk_body(ic, carry):
        sl = pl.ds(pl.multiple_of(ic * c, c), c)
        r = r_ref[0, sl, :]
        lw = lw_ref[0, sl, :]
        k = k_ref[0, sl, :]
        v = v_ref[0, sl, :]
        kk = kk_ref[0, sl, :]
        ab = ab_ref[0, sl, :]
        g = g_ref[0, sl, :]
        cum = _dot_exact_lhs(tri_incl, lw)
        cl = cum[c - 1:c, :]
        e_neg = jnp.exp(-cum)
        e_end = jnp.exp(cl - cum)
        rq = r * jnp.exp(cum)
        kq = kk * jnp.exp(cum - lw)
        bd = ab * e_neg
        kd = k * e_neg
        be = ab * e_end
        ke = k * e_end
        p_end = jnp.exp(cl)
        rkk = r * k * rk_ref[...]

        heads = range(A_HEADS)
        hsl = [slice(h * hd, (h + 1) * hd) for h in heads]
        kq_h = [kq[:, s] for s in hsl]
        rq_h = [rq[:, s] for s in hsl]
        v_hs = [v[:, s] for s in hsl]
        gm = [jnp.where(g_mask,
                        _dot_nt(jnp.concatenate([kq_h[h], rq_h[h]], axis=0),
                                jnp.concatenate([bd[:, hsl[h]], kd[:, hsl[h]]], axis=0)), 0.0)
              for h in heads]
        a_ak = [z[:c, c:] for z in gm]
        a_rb = [z[c:, :c] for z in gm]
        a_rk = [z[c:, c:] for z in gm]
        tinv = _tri_inv_many([z[:c, :c] for z in gm], blk_mask, eye_f)
        wt = [-_dot(t_, k_) for t_, k_ in zip(tinv, kq_h)]
        av = [_dot(a_, v_) for a_, v_ in zip(a_ak, v_hs)]
        ut = [-_dot(t_, a_) for t_, a_ in zip(tinv, av)]
        y0 = [_dot(a_, v_) for a_, v_ in zip(a_rk, v_hs)]
        s0s = [s_ref[h] for h in heads]
        us = [_dot_nt(w_, s_) + u_ for w_, s_, u_ in zip(wt, s0s, ut)]
        y1 = [_dot_nt(r_, s_) for r_, s_ in zip(rq_h, s0s)]
        y2 = [_dot(a_, u_) for a_, u_ in zip(a_rb, us)]
        sn = [_dot_tn(jnp.concatenate([us[h], v_hs[h]], axis=0),
                      jnp.concatenate([be[:, hsl[h]], ke[:, hsl[h]]], axis=0)) for h in heads]
        for h in heads:
            hs = hsl[h]
            v_h = v_hs[h]
            s_ref[h] = s0s[h] * p_end[:, hs] + sn[h]
            y = y1[h] + y2[h] + y0[h]
            ym = jnp.mean(y, axis=-1, keepdims=True)
            yc = y - ym
            yv = jnp.mean(yc * yc, axis=-1, keepdims=True)
            yn = yc * lax.rsqrt(yv + A_GN_EPS) * gng_ref[:, hs] + gnb_ref[:, hs]
            bonus = jnp.sum(rkk[:, hs], axis=-1, keepdims=True) * v_h
            o_ref[0, sl, hs] = (yn + bonus) * g[:, hs]
        return carry

    lax.fori_loop(0, n_chunk, chunk_body, 0)


def _rwkv_core(r, lw, k, v, kk, ab, g, r_k, gn_g, gn_b, tb=256):
    b, t, w = r.shape
    row = lambda z: z.reshape(1, -1)
    blk = pl.BlockSpec((1, tb, w), lambda i, j: (i, j, 0))
    par = pl.BlockSpec((1, w), lambda i, j: (0, 0))
    return pl.pallas_call(
        _rwkv_core_kernel, grid=(b, t // tb), in_specs=[blk] * 7 + [par] * 3, out_specs=blk,
        out_shape=jax.ShapeDtypeStruct((b, t, w), F32),
        scratch_shapes=[pltpu.VMEM((A_HEADS, A_HEAD_DIM, A_HEAD_DIM), F32)],
        compiler_params=_cparams("parallel", "arbitrary"), name="rwkv_core",
    )(r, lw, k, v, kk, ab, g, row(r_k), row(gn_g), row(gn_b))


def _gla_kernel(p_ref, gw2_ref, gb_ref, ng_ref, o_ref, s_ref):
    t = pl.program_id(1)

    @pl.when(t == 0)
    def _():
        s_ref[...] = jnp.zeros_like(s_ref)

    c = CHUNK
    kw, vw = B_KEY_WIDTH, B_VAL_WIDTH
    n_chunk = p_ref.shape[1] // c
    ri = lax.broadcasted_iota(I32, (c, c), 0)
    ci = lax.broadcasted_iota(I32, (c, c), 1)
    causal = ci <= ri
    tri_incl = jnp.where(causal, 1.0, 0.0).astype(BF16)

    def chunk_body(ic, carry):
        sl = pl.ds(pl.multiple_of(ic * c, c), c)
        q = p_ref[0, sl, 0:kw] * (B_KEY_DIM ** -0.5)
        k = p_ref[0, sl, kw:2 * kw]
        v = p_ref[0, sl, 2 * kw:2 * kw + vw]
        g = p_ref[0, sl, 2 * kw + vw:2 * kw + 2 * vw]
        xa = p_ref[0, sl, 2 * kw + 2 * vw:2 * kw + 2 * vw + LANE]
        z = _dot3(xa, gw2_ref[...]) + gb_ref[...]
        log_a = (jnp.minimum(z, 0.0) - jnp.log(1.0 + jnp.exp(-jnp.abs(z)))) * (1.0 / B_GATE_TAU)
        bc = _dot_exact_lhs(tri_incl, log_a)
        bl = bc[c - 1:c, :]
        q_dec = q * jnp.exp(bc)
        k_inv = k * jnp.exp(-bc)
        k_end = k * jnp.exp(bl - bc)
        p_end = jnp.exp(bl)
        heads = range(B_HEADS)
        ksl = [slice(h * B_KEY_DIM, (h + 1) * B_KEY_DIM) for h in heads]
        vsl = [slice(h * B_VAL_DIM, (h + 1) * B_VAL_DIM) for h in heads]
        qd = [q_dec[:, s] for s in ksl]
        v_hs = [v[:, s] for s in vsl]
        att = [jnp.where(causal, _dot_nt(qd[h], k_inv[:, ksl[h]]), 0.0) for h in heads]
        s0s = [s_ref[h] for h in heads]
        o_inter = [_dot_nt(qd[h], s0s[h]) for h in heads]
        o_intra = [_dot(att[h], v_hs[h]) for h in heads]
        sn = [_dot_tn(v_hs[h], k_end[:, ksl[h]]) for h in heads]
        for h in heads:
            s_ref[h] = s0s[h] * p_end[:, ksl[h]] + sn[h]
            o = o_intra[h] + o_inter[h]
            o = o * lax.rsqrt(jnp.mean(o * o, axis=-1, keepdims=True) + B_NORM_EPS)
            g_h = g[:, vsl[h]]
            o_ref[0, sl, vsl[h]] = o * ng_ref[:, vsl[h]] * (g_h * _sigmoid(g_h))
        return carry

    lax.fori_loop(0, n_chunk, chunk_body, 0)


def _gla(pb, gw2p, gate_b, norm_g, tb=256):
    b, t, cols = pb.shape
    return pl.pallas_call(
        _gla_kernel, grid=(b, t // tb),
        in_specs=[pl.BlockSpec((1, tb, cols), lambda i, j: (i, j, 0)),
                  pl.BlockSpec(gw2p.shape, lambda i, j: (0, 0)),
                  pl.BlockSpec((1, B_KEY_WIDTH), lambda i, j: (0, 0)),
                  pl.BlockSpec((1, B_VAL_WIDTH), lambda i, j: (0, 0))],
        out_specs=pl.BlockSpec((1, tb, B_VAL_WIDTH), lambda i, j: (i, j, 0)),
        out_shape=jax.ShapeDtypeStruct((b, t, B_VAL_WIDTH), F32),
        scratch_shapes=[pltpu.VMEM((B_HEADS, B_VAL_DIM, B_KEY_DIM), F32)],
        compiler_params=_cparams("parallel", "arbitrary"), name="gla",
    )(pb, gw2p, gate_b.reshape(1, -1), norm_g.reshape(1, -1))


def _outproj_ln_kernel(*refs):
    n_in = (len(refs) - 7) // 2
    x_ref, g_ref, b_ref, rwt_ref, rb_ref, o_ref, comb_ref = refs[2 * n_in:]
    mix = None
    for y_ref, w_ref in zip(refs[:n_in], refs[n_in:2 * n_in]):
        d = jnp.dot(y_ref[...].astype(BF16), w_ref[...], preferred_element_type=F32)
        mix = d if mix is None else mix + d
    h = _layer_norm(DN_ALPHA * x_ref[...] + mix, g_ref[...], b_ref[...])
    o_ref[...] = h
    _route(h, rwt_ref[...], rb_ref[...], comb_ref)


def _outproj_ln(ys, ws, x2d, g, b, rwt, rbias, tm=256):
    n, d = x2d.shape
    in_specs = [pl.BlockSpec((tm, y.shape[1]), lambda i: (i, 0)) for y in ys]
    in_specs += [pl.BlockSpec(w.shape, lambda i: (0, 0)) for w in ws]
    in_specs += [pl.BlockSpec((tm, d), lambda i: (i, 0)),
                 pl.BlockSpec((1, d), lambda i: (0, 0)), pl.BlockSpec((1, d), lambda i: (0, 0)),
                 pl.BlockSpec((N_EXPERTS, d), lambda i: (0, 0)),
                 pl.BlockSpec((N_EXPERTS, 1), lambda i: (0, 0))]
    return pl.pallas_call(
        _outproj_ln_kernel, grid=(n // tm,), in_specs=in_specs,
        out_specs=[pl.BlockSpec((tm, d), lambda i: (i, 0)),
                   pl.BlockSpec((N_EXPERTS, tm), lambda i: (0, i))],
        out_shape=[jax.ShapeDtypeStruct((n, d), F32), jax.ShapeDtypeStruct((N_EXPERTS, n), F32)],
        compiler_params=_cparams("parallel"), name="out_proj_ln",
    )(*ys, *ws, x2d, g.reshape(1, -1), b.reshape(1, -1), rwt, rbias.reshape(-1, 1))


def _route(h, rwt, rb, o_ref):
    logits = _dot3_nt(rwt, h)
    s = _sigmoid(logits)
    sel = s + rb
    s_rows = [s[e:e + 1, :] for e in range(N_EXPERTS)]
    rows = [sel[e:e + 1, :] for e in range(N_EXPERTS)]
    best_val, best = None, None
    for gidx in range(N_GROUPS):
        mem = rows[gidx * EXPERTS_PER_GROUP:(gidx + 1) * EXPERTS_PER_GROUP]
        gs = None
        for i in range(EXPERTS_PER_GROUP):
            for j in range(i + 1, EXPERTS_PER_GROUP):
                pair = mem[i] + mem[j]
                gs = pair if gs is None else jnp.maximum(gs, pair)
        if best_val is None:
            best_val, best = gs, jnp.zeros(gs.shape, I32)
        else:
            upd = gs > best_val
            best = jnp.where(upd, gidx, best)
            best_val = jnp.where(upd, gs, best_val)
    vals = [jnp.where(best == (e // EXPERTS_PER_GROUP), rows[e], -jnp.inf) for e in range(N_EXPERTS)]

    def arg_top(vs):
        m = functools.reduce(jnp.maximum, vs)
        idx = jnp.full(m.shape, N_EXPERTS, I32)
        for e in reversed(range(N_EXPERTS)):
            idx = jnp.where(vs[e] == m, e, idx)
        return idx

    i1 = arg_top(vals)
    i2 = arg_top([jnp.where(i1 == e, -jnp.inf, vals[e]) for e in range(N_EXPERTS)])
    g1 = functools.reduce(jnp.add, [jnp.where(i1 == e, s_rows[e], 0.0) for e in range(N_EXPERTS)])
    g2 = functools.reduce(jnp.add, [jnp.where(i2 == e, s_rows[e], 0.0) for e in range(N_EXPERTS)])
    tot = g1 + g2
    for e in range(N_EXPERTS):
        o_ref[e:e + 1, :] = jnp.where(i1 == e, g1 / tot, 0.0) + jnp.where(i2 == e, g2 / tot, 0.0)


def _moe_kernel(h_ref, comb_ref, wgu_ref, wd_ref, g_ref, b_ref, o_ref, acc_ref, hb_ref):
    e = pl.program_id(1)

    @pl.when(e == 0)
    def _():
        acc_ref[...] = jnp.zeros_like(acc_ref)
        hb_ref[...] = h_ref[...].astype(BF16)

    gu = jnp.dot(hb_ref[...], wgu_ref[0], preferred_element_type=F32)
    gt, up = gu[:, :D_EXPERT], gu[:, D_EXPERT:]
    comb = comb_ref[...]
    lane = lax.broadcasted_iota(I32, comb.shape, 1)
    ce = jnp.sum(jnp.where(lane == e, comb, 0.0), axis=-1, keepdims=True)
    act = (gt * _sigmoid(gt)) * up * ce
    acc_ref[...] += jnp.dot(act.astype(BF16), wd_ref[0], preferred_element_type=F32)

    @pl.when(e == N_EXPERTS - 1)
    def _():
        o_ref[...] = _layer_norm(DN_ALPHA * h_ref[...] + acc_ref[...], g_ref[...], b_ref[...])


def _moe(h2d, comb, wgu, wd, g, b, tm=1024):
    n, d = h2d.shape
    tm = min(tm, n)
    return pl.pallas_call(
        _moe_kernel, grid=(n // tm, N_EXPERTS),
        in_specs=[pl.BlockSpec((tm, d), lambda i, e: (i, 0)),
                  pl.BlockSpec((tm, N_EXPERTS), lambda i, e: (i, 0)),
                  pl.BlockSpec((1, d, 2 * D_EXPERT), lambda i, e: (e, 0, 0)),
                  pl.BlockSpec((1, D_EXPERT, d), lambda i, e: (e, 0, 0)),
                  pl.BlockSpec((1, d), lambda i, e: (0, 0)),
                  pl.BlockSpec((1, d), lambda i, e: (0, 0))],
        out_specs=pl.BlockSpec((tm, d), lambda i, e: (i, 0)),
        out_shape=jax.ShapeDtypeStruct((n, d), F32),
        scratch_shapes=[pltpu.VMEM((tm, d), F32), pltpu.VMEM((tm, d), BF16)],
        compiler_params=_cparams("parallel", "arbitrary"), name="moe",
    )(h2d, comb, wgu, wd, g.reshape(1, -1), b.reshape(1, -1))


def _rope_full(z, cos, sin_signed):
    return z * cos + pltpu.roll(z, LANE // 2, axis=1) * sin_signed


def _rope_half(z, cos, sin_signed, first_half):
    partner = jnp.where(first_half, pltpu.roll(z, LANE - C_IDX_DIM // 2, axis=1),
                        pltpu.roll(z, C_IDX_DIM // 2, axis=1))
    return z * cos + partner * sin_signed


def _dsa_prep_kernel(p_ref, c128_ref, s128_ref, c64_ref, s64_ref, lng_ref, lnb_ref,
                     q_ref, k_ref, v_ref, iq_ref, ika_ref, ikb_ref, iw_ref):
    c128, s128 = c128_ref[...], s128_ref[...]
    c64, s64 = c64_ref[...], s64_ref[...]
    lane = lax.broadcasted_iota(I32, c64.shape, 1)
    first_half = (lane & (C_IDX_DIM - 1)) < C_IDX_DIM // 2
    for h in range(C_HEADS):
        hs = slice(h * LANE, (h + 1) * LANE)
        q_ref[0, :, hs] = (_rope_full(p_ref[0, :, hs], c128, s128) * Q_SCALE).astype(BF16)
    k0 = C_WIDTH
    k_ref[0] = _rope_full(p_ref[0, :, k0:k0 + LANE], c128, s128).astype(BF16)
    v_ref[0] = p_ref[0, :, k0 + LANE:k0 + 2 * LANE].astype(BF16)
    i0 = k0 + 2 * LANE
    for j in range(C_IDX_HEADS * C_IDX_DIM // LANE):
        js = slice(j * LANE, (j + 1) * LANE)
        z = p_ref[0, :, i0 + j * LANE:i0 + (j + 1) * LANE]
        iq_ref[0, :, js] = (_rope_half(z, c64, s64, first_half) * C_IDX_DIM ** -0.5).astype(BF16)
    t0 = i0 + C_IDX_HEADS * C_IDX_DIM
    tile = p_ref[0, :, t0:t0 + LANE]
    is_key = lane < C_IDX_DIM
    mu = jnp.sum(jnp.where(is_key, tile, 0.0), axis=-1, keepdims=True) * (1.0 / C_IDX_DIM)
    zc = jnp.where(is_key, tile - mu, 0.0)
    var = jnp.sum(zc * zc, axis=-1, keepdims=True) * (1.0 / C_IDX_DIM)
    ikn = zc * lax.rsqrt(var + LN_EPS) * lng_ref[...] + lnb_ref[...]
    ikr = jnp.where(is_key, _rope_half(ikn, c64, s64, first_half), 0.0)
    ika_ref[0] = ikr.astype(BF16)
    ikb_ref[0] = pltpu.roll(ikr, C_IDX_DIM, axis=1).astype(BF16)
    iw_ref[0] = tile * C_IDX_HEADS ** -0.5


def _dsa_prep(p, tabs, lng, lnb, tm=256):
    b, t, cols = p.shape
    tab = pl.BlockSpec((tm, LANE), lambda i, j: (j, 0))
    par = pl.BlockSpec((1, LANE), lambda i, j: (0, 0))
    o = lambda w: pl.BlockSpec((1, tm, w), lambda i, j: (i, j, 0))
    widths = [C_WIDTH, LANE, LANE, C_IDX_HEADS * C_IDX_DIM, LANE, LANE, LANE]
    dtypes = [BF16, BF16, BF16, BF16, BF16, BF16, F32]
    return pl.pallas_call(
        _dsa_prep_kernel, grid=(b, t // tm),
        in_specs=[pl.BlockSpec((1, tm, cols), lambda i, j: (i, j, 0)), tab, tab, tab, tab, par, par],
        out_specs=[o(w) for w in widths],
        out_shape=[jax.ShapeDtypeStruct((b, t, w), dt) for w, dt in zip(widths, dtypes)],
        compiler_params=_cparams("parallel", "parallel"), name="dsa_prep",
    )(p, *tabs, lng, lnb)


def _fold8(z, op):
    return op(z.reshape(z.shape[0] // 8, 8, z.shape[1]), axis=0)


def _dsa_attn_kernel(q_ref, iq_ref, iw_ref, k_ref, vt_ref, ikab_ref, o_ref,
                     sc_ref, tie_ref, bias_ref, s_ref, acc_ref, *, ktop, idx_bits):
    i = pl.program_id(1)
    tq = q_ref.shape[2]
    kg = k_ref.shape[2]
    assert tq == LANE and kg % tq == 0
    ng = i // (kg // tq) + 1
    keypos0 = lax.broadcasted_iota(I32, (kg, tq), 0)
    qpos = i * tq + lax.broadcasted_iota(I32, (kg, tq), 1)

    iqt = iq_ref[0]
    rhs_i = jnp.concatenate([iqt[:LANE], iqt[LANE:]], axis=1)
    w = [iw_ref[0, h:h + 1, :] for h in range(C_IDX_HEADS)]

    def score_body(g, carry):
        sab = jnp.dot(ikab_ref[0, g], rhs_i, preferred_element_type=F32)
        relu = lambda z: jnp.maximum(z, 0.0)
        sc = (relu(sab[:kg, :tq]) * w[0] + relu(sab[kg:, :tq]) * w[1]
              + relu(sab[:kg, tq:]) * w[2] + relu(sab[kg:, tq:]) * w[3]) + 0.0
        sc_ref[g] = jnp.where(g * kg + keypos0 <= qpos, sc, -jnp.inf)
        return carry

    lax.fori_loop(0, ng, score_body, 0)

    def count(ref, pred):
        def body(g, acc):
            return acc + _fold8(jnp.where(pred(ref[g]), 1, 0), jnp.sum)
        acc = lax.fori_loop(0, ng, body, jnp.zeros((8, tq), I32))
        return jnp.sum(acc, axis=0, keepdims=True)

    def key_to_float(key):
        return pltpu.bitcast(key ^ ((key >> 31) & 0x7FFFFFFF), F32)

    def bisect(it, carry):
        tkey, c_lo = carry
        ckey = tkey + lax.shift_left(jnp.int32(1), 31 - it)
        cand = key_to_float(ckey)
        cnt = jnp.where(ckey <= KEY_NEG_INF, ng * kg, count(sc_ref, lambda x: x >= cand))
        take = cnt >= ktop
        return jnp.where(take, ckey, tkey), jnp.where(take, cnt, c_lo)

    tkey, c_lo = lax.fori_loop(0, 32, bisect, (jnp.full((1, tq), INT_MIN, I32),
                                               jnp.zeros((1, tq), I32) + ng * kg))
    tau = key_to_float(tkey)

    tie_rows = jnp.where(c_lo > ktop, jnp.where(tau >= F32_LOWEST, 1, 0), 0)
    need_tie = jnp.max(tie_rows)

    @pl.when(need_tie == 0)
    def _():
        thr = jnp.maximum(tau, F32_LOWEST)

        def body(g, carry):
            bias_ref[g] = jnp.where(sc_ref[g] >= thr, 0.0, NEG_BIG)
            return carry

        lax.fori_loop(0, ng, body, 0)

    @pl.when(need_tie > 0)
    def _():
        need = ktop - count(sc_ref, lambda x: x > tau)

        def tie_body(g, carry):
            sc = sc_ref[g]
            idx = jnp.where(sc >= F32_LOWEST, g * kg + keypos0, TIE_NONE)
            tie_ref[g] = jnp.where(sc == tau, idx, TIE_NONE)
            return carry

        lax.fori_loop(0, ng, tie_body, 0)

        def bisect_idx(it, ans):
            cand = ans + lax.shift_left(jnp.int32(1), idx_bits - 1 - it)
            cnt = count(tie_ref, lambda x: x < cand)
            return jnp.where(cnt < need, cand, ans)

        jstar = lax.fori_loop(0, idx_bits, bisect_idx, jnp.zeros((1, tq), I32))
        thr = jnp.maximum(tau, F32_LOWEST)

        def body(g, carry):
            sel = jnp.where(sc_ref[g] > thr, 1, jnp.where(tie_ref[g] <= jstar, 1, 0))
            bias_ref[g] = jnp.where(sel > 0, 0.0, NEG_BIG)
            return carry

        lax.fori_loop(0, ng, body, 0)

    qt = q_ref[0]
    pairs = range(C_HEADS // 2)
    q_cols = [jnp.concatenate([qt[(2 * j) * LANE:(2 * j + 1) * LANE],
                               qt[(2 * j + 1) * LANE:(2 * j + 2) * LANE]], axis=1) for j in pairs]

    def pass1(g, m8s):
        k_g = k_ref[0, g]
        bias2 = jnp.concatenate([bias_ref[g]] * 2, axis=1)
        out = []
        for j in pairs:
            s2 = jnp.dot(k_g, q_cols[j], preferred_element_type=F32) + bias2
            s_ref[j, g] = s2
            out.append(jnp.maximum(m8s[j], _fold8(s2, jnp.max)))
        return tuple(out)

    m8s = lax.fori_loop(0, ng, pass1, tuple(jnp.full((8, 2 * tq), NEG_BIG, F32) for _ in pairs))
    ms = [jnp.max(m8, axis=0, keepdims=True) for m8 in m8s]
    acc_ref[...] = jnp.zeros_like(acc_ref)

    def pass2(g, l8s):
        vt_g = vt_ref[0, g]
        out = []
        for j in pairs:
            p2 = jnp.exp2(s_ref[j, g] - ms[j])
            acc_ref[j] += jnp.dot(vt_g, p2.astype(BF16), preferred_element_type=F32)
            out.append(l8s[j] + _fold8(p2, jnp.sum))
        return tuple(out)

    l8s = lax.fori_loop(0, ng, pass2, tuple(jnp.zeros((8, 2 * tq), F32) for _ in pairs))
    for j in pairs:
        out2 = acc_ref[j] / jnp.sum(l8s[j], axis=0, keepdims=True)
        for u in range(2):
            h = 2 * j + u
            o_ref[0, :, h * LANE:(h + 1) * LANE] = out2[:, u * tq:(u + 1) * tq].T


def _dsa_attn(qt, iqt, iwt, kg4, vtg4, ikab4, ktop):
    b, _, t = qt.shape
    tq = LANE
    ngt, kg = kg4.shape[1], kg4.shape[2]
    qspec = lambda w: pl.BlockSpec((1, tq, w), lambda i, j: (i, j, 0))
    whole = lambda z: pl.BlockSpec((1,) + z.shape[1:], lambda i, j: (i, 0, 0, 0))
    kern = functools.partial(_dsa_attn_kernel, ktop=ktop, idx_bits=int(math.log2(t)))
    scratch = [pltpu.VMEM((ngt, kg, tq), F32), pltpu.VMEM((ngt, kg, tq), I32),
               pltpu.VMEM((ngt, kg, tq), F32), pltpu.VMEM((C_HEADS // 2, ngt, kg, 2 * tq), F32),
               pltpu.VMEM((C_HEADS // 2, C_HEAD_DIM, 2 * tq), F32)]
    return pl.pallas_call(
        kern, grid=(b, t // tq),
        in_specs=[pl.BlockSpec((1, C_WIDTH, tq), lambda i, j: (i, 0, j)),
                  pl.BlockSpec((1, C_IDX_HEADS * C_IDX_DIM, tq), lambda i, j: (i, 0, j)),
                  pl.BlockSpec((1, C_IDX_HEADS, tq), lambda i, j: (i, 0, j)),
                  whole(kg4), whole(vtg4), whole(ikab4)],
        out_specs=qspec(C_WIDTH),
        out_shape=jax.ShapeDtypeStruct((b, t, C_WIDTH), F32),
        scratch_shapes=scratch,
        compiler_params=_cparams("parallel", "arbitrary"), name="dsa_attn",
    )(qt, iqt, iwt, kg4, vtg4, ikab4)


def _rope_tables(t):
    pos = jnp.arange(t, dtype=F32)

    def tab(dim):
        half = dim // 2
        inv = ROPE_THETA ** (-jnp.arange(half, dtype=F32) / half)
        ang = pos[:, None] * inv[None, :]
        cos, sin = jnp.cos(ang), jnp.sin(ang)
        reps = LANE // dim
        return (jnp.tile(jnp.concatenate([cos, cos], axis=-1), (1, reps)),
                jnp.tile(jnp.concatenate([-sin, sin], axis=-1), (1, reps)))

    c128, s128 = tab(C_HEAD_DIM)
    c64, s64 = tab(C_IDX_DIM)
    return c128, s128, c64, s64


def _pad_cols(w, cols):
    return jnp.pad(w, ((0, 0), (0, cols - w.shape[1])))


def _moe_block(h, comb_t, w_gate, w_up, w_down, g, b):
    wgu = jnp.concatenate([w_gate, w_up], axis=-1).astype(BF16)
    return _moe(h, comb_t.T, wgu, w_down.astype(BF16), g, b)


@jax.jit
def _forward(x, w_in_even, a_mu, a_w0, a_w2, a_a0, a_a2, a_g2, a_kk_scale, a_ka_scale, a_r_k,
             a_gn_g, a_gn_b, b_gate_w2, b_gate_b, b_norm_g, w_out_even, w_in_odd, c_ik_ln_g,
             c_ik_ln_b, w_out_odd, ln1_g, ln1_b, ln2_g, ln2_b, router_w, router_bias,
             exp_w_gate, exp_w_up, exp_w_down):
    bsz, t, d = x.shape
    assert d == D_MODEL and t % 256 == 0 and (t & (t - 1)) == 0
    n = bsz * t
    ktop = min(C_INDEX_TOPK, t // 4)
    rwt = router_w.T
    xf = x.reshape(n, d)
    hsum = (jnp.arange(A_WIDTH)[:, None] // A_HEAD_DIM
            == jnp.arange(A_WIDTH)[None, :] // A_HEAD_DIM).astype(BF16)

    for l in range(DEPTH):
        i = l // 2
        if l % 2 == 0:
            w = w_in_even[i]
            pa, pb = _proj(xf, [w[:, :A_COLS].astype(BF16),
                                _pad_cols(w[:, A_COLS:], B_PAD_COLS).astype(BF16)])
            wwa = jnp.zeros((LANE, 2 * A_WIDTH), F32)
            wwa = wwa.at[:A_DECAY_LORA, :A_WIDTH].set(a_w2[i]).at[A_DECAY_LORA:, A_WIDTH:].set(a_a2[i])
            parts = _rwkv_prep(pa.reshape(bsz, t, A_COLS), a_mu[i], wwa, a_w0[i], a_a0[i], a_g2[i],
                               a_kk_scale[i], a_ka_scale[i], hsum)
            ya = _rwkv_core(*parts, a_r_k[i].reshape(-1), a_gn_g[i], a_gn_b[i])
            gw2p = jnp.zeros((LANE, B_KEY_WIDTH), F32).at[:B_GATE_LORA].set(b_gate_w2[i])
            yb = _gla(pb.reshape(bsz, t, B_PAD_COLS), gw2p, b_gate_b[i], b_norm_g[i])
            wo = w_out_even[i].astype(BF16)
            h, comb_t = _outproj_ln([ya.reshape(n, A_WIDTH), yb.reshape(n, B_VAL_WIDTH)],
                                    [wo[:A_WIDTH], wo[A_WIDTH:]], xf, ln1_g[l], ln1_b[l],
                                    rwt, router_bias)
        else:
            (p,) = _proj(xf, [_pad_cols(w_in_odd[i], ODD_PAD_COLS).astype(BF16)])
            pad = lambda z: jnp.pad(z, (0, LANE - z.shape[0])).reshape(1, LANE)
            q, k, v, iq, ika, ikb, iw = _dsa_prep(p.reshape(bsz, t, ODD_PAD_COLS), _rope_tables(t),
                                                  pad(c_ik_ln_g[i]), pad(c_ik_ln_b[i]))
            kgrp = min(KEY_GROUP, t)
            grp = lambda z: z.reshape(bsz, t // kgrp, kgrp, LANE)
            iwt = jnp.swapaxes(iw[:, :, C_IDX_DIM:C_IDX_DIM + C_IDX_HEADS], 1, 2)
            att = _dsa_attn(jnp.swapaxes(q, 1, 2), jnp.swapaxes(iq, 1, 2), iwt, grp(k),
                            jnp.swapaxes(grp(v), 2, 3),
                            jnp.concatenate([grp(ika), grp(ikb)], axis=2), ktop)
            h, comb_t = _outproj_ln([att.reshape(n, C_WIDTH)], [w_out_odd[i].astype(BF16)], xf,
                                    ln1_g[l], ln1_b[l], rwt, router_bias)
        xf = _moe_block(h, comb_t, exp_w_gate[l], exp_w_up[l], exp_w_down[l], ln2_g[l], ln2_b[l])
    return xf.reshape(bsz, t, d)


def kernel(x, w_in_even, a_mu, a_w0, a_w2, a_a0, a_a2, a_g2, a_kk_scale, a_ka_scale, a_r_k, a_gn_g, a_gn_b, b_gate_w2, b_gate_b, b_norm_g, w_out_even, w_in_odd, c_ik_ln_g, c_ik_ln_b, w_out_odd, ln1_g, ln1_b, ln2_g, ln2_b, router_w, router_bias, exp_w_gate, exp_w_up, exp_w_down):
    return _forward(x, w_in_even, a_mu, a_w0, a_w2, a_a0, a_a2, a_g2, a_kk_scale, a_ka_scale, a_r_k,
                    a_gn_g, a_gn_b, b_gate_w2, b_gate_b, b_norm_g, w_out_even, w_in_odd, c_ik_ln_g,
                    c_ik_ln_b, w_out_odd, ln1_g, ln1_b, ln2_g, ln2_b, router_w, router_bias,
                    exp_w_gate, exp_w_up, exp_w_down)
```

```python
import functools
import math

import numpy as np
import jax
import jax.numpy as jnp
from jax import lax
from jax.experimental import pallas as pl
from jax.experimental.pallas import tpu as pltpu

F32 = jnp.float32
BF16 = jnp.bfloat16
I32 = jnp.int32

D_MODEL = 1024
DEPTH = 2
A_HEADS, A_HEAD_DIM = 8, 64
A_WIDTH = A_HEADS * A_HEAD_DIM
A_DECAY_LORA, A_ICLR_LORA, A_GATE_LORA = 64, 64, 128
A_GN_EPS = 64e-5
A_COLS = 3 * A_WIDTH + A_DECAY_LORA + A_ICLR_LORA + A_GATE_LORA
B_HEADS, B_KEY_DIM, B_VAL_DIM = 4, 64, 128
B_KEY_WIDTH = B_HEADS * B_KEY_DIM
B_VAL_WIDTH = B_HEADS * B_VAL_DIM
B_GATE_LORA = 16
B_GATE_TAU = 16.0
B_NORM_EPS = 1e-5
B_COLS = 2 * B_KEY_WIDTH + 2 * B_VAL_WIDTH + B_GATE_LORA
C_HEADS, C_HEAD_DIM = 8, 128
C_WIDTH = C_HEADS * C_HEAD_DIM
C_IDX_HEADS, C_IDX_DIM = 4, 64
C_INDEX_TOPK = 256
ODD_COLS = C_WIDTH + 2 * C_HEAD_DIM + C_IDX_HEADS * C_IDX_DIM + C_IDX_DIM + C_IDX_HEADS
ROPE_THETA = 10000.0
N_EXPERTS, N_GROUPS, TOP_K, D_EXPERT = 16, 4, 2, 256
EXPERTS_PER_GROUP = N_EXPERTS // N_GROUPS
DN_ALPHA = (2 * DEPTH) ** 0.25
LN_EPS = 1e-5

LANE = 128
CHUNK = 64
INV_BLOCK = 16
VMEM_LIMIT = 56 * 1024 * 1024
B_PAD_COLS = 13 * LANE
ODD_PAD_COLS = 13 * LANE
INT_MIN = -2 ** 31
NEG_BIG = -1e30
TIE_NONE = 2 ** 30
F32_LOWEST = float(np.finfo(np.float32).min)
KEY_NEG_INF = INT_MIN + 0x7FFFFF
KEY_GROUP = 4 * LANE
Q_SCALE = math.log2(math.e) * C_HEAD_DIM ** -0.5


def _cparams(*sem):
    return pltpu.CompilerParams(dimension_semantics=sem, vmem_limit_bytes=VMEM_LIMIT)


def _dot(a, b):
    return jnp.dot(a.astype(BF16), b.astype(BF16), preferred_element_type=F32)


def _dot_nt(a, b):
    return lax.dot_general(a.astype(BF16), b.astype(BF16), (((1,), (1,)), ((), ())),
                           preferred_element_type=F32)


def _dot_tn(a, b):
    return lax.dot_general(a.astype(BF16), b.astype(BF16), (((0,), (0,)), ((), ())),
                           preferred_element_type=F32)


def _split2(a):
    hi = a.astype(BF16)
    lo = (a - hi.astype(F32)).astype(BF16)
    return hi, lo


def _split3(a):
    hi = a.astype(BF16)
    r1 = a - hi.astype(F32)
    mid = r1.astype(BF16)
    lo = (r1 - mid.astype(F32)).astype(BF16)
    return hi, mid, lo


def _dot_exact_lhs(l_bf16, a):
    hi, mid, lo = _split3(a)
    d = lambda z: jnp.dot(l_bf16, z, preferred_element_type=F32)
    return d(hi) + d(mid) + d(lo)


def _dot_exact_rhs(a, r_bf16):
    hi, mid, lo = _split3(a)
    d = lambda z: jnp.dot(z, r_bf16, preferred_element_type=F32)
    return d(hi) + d(mid) + d(lo)


def _dot3(a, b):
    ah, al = _split2(a)
    bh, bl = _split2(b)
    d = lambda x, y: jnp.dot(x, y, preferred_element_type=F32)
    return d(ah, bh) + d(ah, bl) + d(al, bh)


def _dot3_nt(a, b):
    ah, al = _split2(a)
    bh, bl = _split2(b)
    d = lambda x, y: lax.dot_general(x, y, (((1,), (1,)), ((), ())), preferred_element_type=F32)
    return d(ah, bh) + d(ah, bl) + d(al, bh)


def _sigmoid(z):
    return 1.0 / (1.0 + jnp.exp(-z))


def _layer_norm(z, g, b):
    mu = jnp.mean(z, axis=-1, keepdims=True)
    zc = z - mu
    var = jnp.mean(zc * zc, axis=-1, keepdims=True)
    return zc * lax.rsqrt(var + LN_EPS) * g + b


def _proj_kernel(x_ref, *refs):
    n_out = len(refs) // 2
    xb = x_ref[...].astype(BF16)
    for w_ref, o_ref in zip(refs[:n_out], refs[n_out:]):
        o_ref[...] = jnp.dot(xb, w_ref[...], preferred_element_type=F32)


def _proj(x2d, ws, tm=256):
    n, d = x2d.shape
    in_specs = [pl.BlockSpec((tm, d), lambda i: (i, 0))]
    in_specs += [pl.BlockSpec(w.shape, lambda i: (0, 0)) for w in ws]
    out_specs = [pl.BlockSpec((tm, w.shape[1]), lambda i: (i, 0)) for w in ws]
    out_shape = [jax.ShapeDtypeStruct((n, w.shape[1]), F32) for w in ws]
    return pl.pallas_call(
        _proj_kernel, grid=(n // tm,), in_specs=in_specs, out_specs=out_specs,
        out_shape=out_shape, compiler_params=_cparams("parallel"), name="in_proj",
    )(x2d, *ws)


def _rwkv_front(p, prev, mu, wwa, w0, a0, g2, kks_scale, kas_scale, hsum):
    row = lax.broadcasted_iota(I32, p.shape, 0)
    shifted = jnp.where(row == 0, prev, pltpu.roll(p, 1, axis=0))
    pm = p + (shifted - p) * mu

    w = A_WIDTH
    r = pm[:, 0:w]
    k = pm[:, w:2 * w]
    v = pm[:, 2 * w:3 * w]
    xwa = pm[:, 3 * w:3 * w + LANE]
    xg = pm[:, 3 * w + LANE:3 * w + 2 * LANE]

    lane = lax.broadcasted_iota(I32, xwa.shape, 1)
    lhs = jnp.where(lane < A_DECAY_LORA, jnp.tanh(xwa), xwa)
    z = _dot3(lhs, wwa)
    lw = -_sigmoid(z[:, 0:w] + w0) * math.exp(-0.5)
    a = _sigmoid(z[:, w:2 * w] + a0)
    g = _dot(_sigmoid(xg), g2)

    kks = k * kks_scale
    ss = _dot_exact_rhs(kks * kks, hsum)
    kkn = kks * lax.rsqrt(jnp.maximum(ss, 1e-24))
    kh = k * (1.0 + (a - 1.0) * kas_scale)
    return r, lw, kh, v, kkn, kkn * a, g


HEAD_GROUP = 4
GROUP_W = HEAD_GROUP * A_HEAD_DIM


def _bdx(y, bd_mask):
    return jnp.where(bd_mask, jnp.concatenate([y] * HEAD_GROUP, axis=0), 0.0).astype(BF16)


def _tri_inv_groups(mats, blk_mask, eye_l, bd_mask):
    mm = lambda x, y: jnp.dot(x.astype(BF16), _bdx(y, bd_mask), preferred_element_type=F32)
    assert CHUNK // INV_BLOCK == 4
    dg = [jnp.where(blk_mask, a, 0.0) for a in mats]
    e = [a - d for a, d in zip(mats, dg)]
    pw = [-d for d in dg]
    dinv = [eye_l + p for p in pw]
    for _ in range(int(math.log2(INV_BLOCK)) - 1):
        pw = [mm(p, p) for p in pw]
        dinv = [mm(d, eye_l + p) for d, p in zip(dinv, pw)]
    f = [mm(d, e_) for d, e_ in zip(dinv, e)]
    m = [d - mm(f_, d) for d, f_ in zip(dinv, f)]
    f2 = [mm(f_, f_) for f_ in f]
    return [m_ + mm(f2_, m_) for m_, f2_ in zip(m, f2)]


def _rwkv_core_kernel(p_ref, mu_ref, wwa_ref, w0_ref, a0_ref, g2_ref, kks_ref, kas_ref, hsum_ref,
                      rk_ref, gng_ref, gnb_ref, o_ref, s_ref, carry_ref):
    t = pl.program_id(1)

    @pl.when(t == 0)
    def _():
        s_ref[...] = jnp.zeros_like(s_ref)
        carry_ref[...] = jnp.zeros_like(carry_ref)

    p = p_ref[0]
    r_all, lw_all, k_all, v_all, kk_all, ab_all, g_all = _rwkv_front(
        p, carry_ref[...], mu_ref[...], wwa_ref[...], w0_ref[...], a0_ref[...], g2_ref[...],
        kks_ref[...], kas_ref[...], hsum_ref[...])
    carry_ref[...] = p[p.shape[0] - 1:p.shape[0], :]

    c = CHUNK
    hd = A_HEAD_DIM
    gw = GROUP_W
    n_chunk = p_ref.shape[1] // c
    groups = range(A_WIDTH // gw)
    ri = lax.broadcasted_iota(I32, (c, c), 0)
    ci = lax.broadcasted_iota(I32, (c, c), 1)
    tri_incl = jnp.where(ci <= ri, 1.0, 0.0).astype(BF16)
    lt = lax.broadcasted_iota(I32, (c, gw), 0)
    ls = lax.broadcasted_iota(I32, (c, gw), 1) & (hd - 1)
    eye_l = jnp.where(ls == lt, 1.0, 0.0).astype(F32)
    blk_mask = (lt // INV_BLOCK) == (ls // INV_BLOCK)
    gt = lax.broadcasted_iota(I32, (2 * c, gw), 0)
    gs = lax.broadcasted_iota(I32, (2 * c, gw), 1) & (hd - 1)
    g_mask = gs < (gt & (c - 1)) + jnp.where(gt >= c, 1, 0)
    bi = lax.broadcasted_iota(I32, (gw, gw), 0)
    bj = lax.broadcasted_iota(I32, (gw, gw), 1)
    bd_mask = (bi // hd) == (bj // hd)
    mm = lambda x, y: jnp.dot(x.astype(BF16), _bdx(y, bd_mask), preferred_element_type=F32)
    nt = lambda x, y: lax.dot_general(x, _bdx(y, bd_mask), (((1,), (1,)), ((), ())),
                                      preferred_element_type=F32)
    chunks = range(n_chunk)
    gsl = [slice(i * gw, (i + 1) * gw) for i in groups]
    probs = [(ic, i) for ic in chunks for i in groups]

    pre = []
    for ic in chunks:
        rows = slice(ic * c, (ic + 1) * c)
        r, lw, k, v = r_all[rows], lw_all[rows], k_all[rows], v_all[rows]
        kk, ab = kk_all[rows], ab_all[rows]
        cum = _dot_exact_lhs(tri_incl, lw)
        cl = cum[c - 1:c, :]
        e_neg = jnp.exp(-cum)
        e_end = jnp.exp(cl - cum)
        pre.append(dict(
            v=v, rq=r * jnp.exp(cum), kq=kk * jnp.exp(cum - lw),
            bd=ab * e_neg, kd=k * e_neg, be=ab * e_end, ke=k * e_end,
            p_end=jnp.exp(cl), rkk=r * k * rk_ref[...]))
    lhs = {p: jnp.concatenate([pre[p[0]]["kq"][:, gsl[p[1]]], pre[p[0]]["rq"][:, gsl[p[1]]]],
                              axis=0).astype(BF16) for p in probs}
    gb = {p: jnp.where(g_mask, nt(lhs[p], pre[p[0]]["bd"][:, gsl[p[1]]]), 0.0) for p in probs}
    gk = {p: jnp.where(g_mask, nt(lhs[p], pre[p[0]]["kd"][:, gsl[p[1]]]), 0.0) for p in probs}
    tinv = dict(zip(probs, _tri_inv_groups([gb[p][:c] for p in probs], blk_mask, eye_l, bd_mask)))
    avy = {p: mm(gk[p], pre[p[0]]["v"][:, gsl[p[1]]]) for p in probs}
    wt = {p: -mm(tinv[p], pre[p[0]]["kq"][:, gsl[p[1]]]) for p in probs}
    ut = {p: -mm(tinv[p], avy[p][:c]) for p in probs}

    for ic in chunks:
        sl = pl.ds(ic * c, c)
        d = pre[ic]
        g = g_all[ic * c:(ic + 1) * c]
        s0s = [s_ref[i] for i in groups]
        uy = [_dot_nt(jnp.concatenate([wt[ic, i], lhs[ic, i][c:]], axis=0), s0s[i]) for i in groups]
        us = [uy[i][:c] + ut[ic, i] for i in groups]
        y2 = [mm(gb[ic, i][c:], us[i]) for i in groups]
        sn = [_dot_tn(jnp.concatenate([us[i], d["v"][:, gsl[i]]], axis=0),
                      jnp.concatenate([d["be"][:, gsl[i]], d["ke"][:, gsl[i]]], axis=0)) for i in groups]
        for i in groups:
            s_ref[i] = s0s[i] * d["p_end"][:, gsl[i]] + jnp.where(bd_mask, sn[i], 0.0)
            y_g = uy[i][c:] + y2[i] + avy[ic, i][c:]
            for j in range(HEAD_GROUP):
                hs = slice(i * gw + j * hd, i * gw + (j + 1) * hd)
                y = y_g[:, j * hd:(j + 1) * hd]
                ym = jnp.mean(y, axis=-1, keepdims=True)
                yc = y - ym
                yv = jnp.mean(yc * yc, axis=-1, keepdims=True)
                yn = yc * lax.rsqrt(yv + A_GN_EPS) * gng_ref[:, hs] + gnb_ref[:, hs]
                bonus = jnp.sum(d["rkk"][:, hs], axis=-1, keepdims=True) * d["v"][:, hs]
                o_ref[0, sl, hs] = (yn + bonus) * g[:, hs]


def _rwkv(pa, mu, wwa, w0, a0, g2, kks, kas, hsum, r_k, gn_g, gn_b, tb=256):
    b, t, _ = pa.shape
    w = A_WIDTH
    row = lambda z: z.reshape(1, -1)
    full = lambda z: pl.BlockSpec(z.shape, lambda i, j: (0,) * z.ndim)
    args = [pa, row(mu), wwa, row(w0), row(a0), g2, row(kks), row(kas), hsum,
            row(r_k), row(gn_g), row(gn_b)]
    in_specs = [pl.BlockSpec((1, tb, A_COLS), lambda i, j: (i, j, 0))] + [full(z) for z in args[1:]]
    return pl.pallas_call(
        _rwkv_core_kernel, grid=(b, t // tb), in_specs=in_specs,
        out_specs=pl.BlockSpec((1, tb, w), lambda i, j: (i, j, 0)),
        out_shape=jax.ShapeDtypeStruct((b, t, w), F32),
        scratch_shapes=[pltpu.VMEM((A_WIDTH // GROUP_W, GROUP_W, GROUP_W), F32),
                        pltpu.VMEM((1, A_COLS), F32)],
        compiler_params=_cparams("parallel", "arbitrary"), name="rwkv",
    )(*args)


def _gla_kernel(p_ref, gw2_ref, gb_ref, ng_ref, o_ref, s_ref):
    t = pl.program_id(1)

    @pl.when(t == 0)
    def _():
        s_ref[...] = jnp.zeros_like(s_ref)

    c = CHUNK
    kw, vw = B_KEY_WIDTH, B_VAL_WIDTH
    n_chunk = p_ref.shape[1] // c
    ri = lax.broadcasted_iota(I32, (c, c), 0)
    ci = lax.broadcasted_iota(I32, (c, c), 1)
    causal = ci <= ri
    tri_incl = jnp.where(causal, 1.0, 0.0).astype(BF16)

    chunks = range(n_chunk)
    heads = range(B_HEADS)
    ksl = [slice(h * B_KEY_DIM, (h + 1) * B_KEY_DIM) for h in heads]
    vsl = [slice(h * B_VAL_DIM, (h + 1) * B_VAL_DIM) for h in heads]
    probs = [(ic, h) for ic in chunks for h in heads]

    pre = []
    for ic in chunks:
        sl = pl.ds(ic * c, c)
        q = p_ref[0, sl, 0:kw] * (B_KEY_DIM ** -0.5)
        k = p_ref[0, sl, kw:2 * kw]
        xa = p_ref[0, sl, 2 * kw + 2 * vw:2 * kw + 2 * vw + LANE]
        z = _dot3(xa, gw2_ref[...]) + gb_ref[...]
        log_a = (jnp.minimum(z, 0.0) - jnp.log(1.0 + jnp.exp(-jnp.abs(z)))) * (1.0 / B_GATE_TAU)
        bc = _dot_exact_lhs(tri_incl, log_a)
        bl = bc[c - 1:c, :]
        pre.append(dict(q_dec=q * jnp.exp(bc), k_inv=k * jnp.exp(-bc), k_end=k * jnp.exp(bl - bc),
                        p_end=jnp.exp(bl), v=p_ref[0, sl, 2 * kw:2 * kw + vw]))
    qd = {(ic, h): pre[ic]["q_dec"][:, ksl[h]] for ic, h in probs}
    v_h = {(ic, h): pre[ic]["v"][:, vsl[h]] for ic, h in probs}
    att = {p: jnp.where(causal, _dot_nt(qd[p], pre[p[0]]["k_inv"][:, ksl[p[1]]]), 0.0) for p in probs}
    o_intra = {p: _dot(att[p], v_h[p]) for p in probs}
    sn = {p: _dot_tn(v_h[p], pre[p[0]]["k_end"][:, ksl[p[1]]]) for p in probs}

    for ic in chunks:
        sl = pl.ds(ic * c, c)
        g = p_ref[0, sl, 2 * kw + vw:2 * kw + 2 * vw]
        s0s = [s_ref[h] for h in heads]
        o_inter = [_dot_nt(qd[ic, h], s0s[h]) for h in heads]
        for h in heads:
            s_ref[h] = s0s[h] * pre[ic]["p_end"][:, ksl[h]] + sn[ic, h]
            o = o_intra[ic, h] + o_inter[h]
            o = o * lax.rsqrt(jnp.mean(o * o, axis=-1, keepdims=True) + B_NORM_EPS)
            g_h = g[:, vsl[h]]
            o_ref[0, sl, vsl[h]] = o * ng_ref[:, vsl[h]] * (g_h * _sigmoid(g_h))


def _gla(pb, gw2p, gate_b, norm_g, tb=256):
    b, t, cols = pb.shape
    return pl.pallas_call(
        _gla_kernel, grid=(b, t // tb),
        in_specs=[pl.BlockSpec((1, tb, cols), lambda i, j: (i, j, 0)),
                  pl.BlockSpec(gw2p.shape, lambda i, j: (0, 0)),
                  pl.BlockSpec((1, B_KEY_WIDTH), lambda i, j: (0, 0)),
                  pl.BlockSpec((1, B_VAL_WIDTH), lambda i, j: (0, 0))],
        out_specs=pl.BlockSpec((1, tb, B_VAL_WIDTH), lambda i, j: (i, j, 0)),
        out_shape=jax.ShapeDtypeStruct((b, t, B_VAL_WIDTH), F32),
        scratch_shapes=[pltpu.VMEM((B_HEADS, B_VAL_DIM, B_KEY_DIM), F32)],
        compiler_params=_cparams("parallel", "arbitrary"), name="gla",
    )(pb, gw2p, gate_b.reshape(1, -1), norm_g.reshape(1, -1))


def _outproj_ln_kernel(*refs):
    n_in = (len(refs) - 7) // 2
    x_ref, g_ref, b_ref, rwt_ref, rb_ref, o_ref, comb_ref = refs[2 * n_in:]
    mix = None
    for y_ref, w_ref in zip(refs[:n_in], refs[n_in:2 * n_in]):
        d = jnp.dot(y_ref[...].astype(BF16), w_ref[...], preferred_element_type=F32)
        mix = d if mix is None else mix + d
    h = _layer_norm(DN_ALPHA * x_ref[...] + mix, g_ref[...], b_ref[...])
    o_ref[...] = h
    _route(h, rwt_ref[...], rb_ref[...], comb_ref)


def _outproj_ln(ys, ws, x2d, g, b, rwt, rbias, tm=256):
    n, d = x2d.shape
    in_specs = [pl.BlockSpec((tm, y.shape[1]), lambda i: (i, 0)) for y in ys]
    in_specs += [pl.BlockSpec(w.shape, lambda i: (0, 0)) for w in ws]
    in_specs += [pl.BlockSpec((tm, d), lambda i: (i, 0)),
                 pl.BlockSpec((1, d), lambda i: (0, 0)), pl.BlockSpec((1, d), lambda i: (0, 0)),
                 pl.BlockSpec((N_EXPERTS, d), lambda i: (0, 0)),
                 pl.BlockSpec((N_EXPERTS, 1), lambda i: (0, 0))]
    return pl.pallas_call(
        _outproj_ln_kernel, grid=(n // tm,), in_specs=in_specs,
        out_specs=[pl.BlockSpec((tm, d), lambda i: (i, 0)),
                   pl.BlockSpec((N_EXPERTS, tm), lambda i: (0, i))],
        out_shape=[jax.ShapeDtypeStruct((n, d), F32), jax.ShapeDtypeStruct((N_EXPERTS, n), F32)],
        compiler_params=_cparams("parallel"), name="out_proj_ln",
    )(*ys, *ws, x2d, g.reshape(1, -1), b.reshape(1, -1), rwt, rbias.reshape(-1, 1))


def _route(h, rwt, rb, o_ref):
    logits = _dot3_nt(rwt, h)
    s = _sigmoid(logits)
    sel = s + rb
    s_rows = [s[e:e + 1, :] for e in range(N_EXPERTS)]
    rows = [sel[e:e + 1, :] for e in range(N_EXPERTS)]
    best_val, best = None, None
    for gidx in range(N_GROUPS):
        mem = rows[gidx * EXPERTS_PER_GROUP:(gidx + 1) * EXPERTS_PER_GROUP]
        gs = None
        for i in range(EXPERTS_PER_GROUP):
            for j in range(i + 1, EXPERTS_PER_GROUP):
                pair = mem[i] + mem[j]
                gs = pair if gs is None else jnp.maximum(gs, pair)
        if best_val is None:
            best_val, best = gs, jnp.zeros(gs.shape, I32)
        else:
            upd = gs > best_val
            best = jnp.where(upd, gidx, best)
            best_val = jnp.where(upd, gs, best_val)
    vals = [jnp.where(best == (e // EXPERTS_PER_GROUP), rows[e], -jnp.inf) for e in range(N_EXPERTS)]

    def arg_top(vs):
        m = functools.reduce(jnp.maximum, vs)
        idx = jnp.full(m.shape, N_EXPERTS, I32)
        for e in reversed(range(N_EXPERTS)):
            idx = jnp.where(vs[e] == m, e, idx)
        return idx

    i1 = arg_top(vals)
    i2 = arg_top([jnp.where(i1 == e, -jnp.inf, vals[e]) for e in range(N_EXPERTS)])
    g1 = functools.reduce(jnp.add, [jnp.where(i1 == e, s_rows[e], 0.0) for e in range(N_EXPERTS)])
    g2 = functools.reduce(jnp.add, [jnp.where(i2 == e, s_rows[e], 0.0) for e in range(N_EXPERTS)])
    tot = g1 + g2
    for e in range(N_EXPERTS):
        o_ref[e:e + 1, :] = jnp.where(i1 == e, g1 / tot, 0.0) + jnp.where(i2 == e, g2 / tot, 0.0)


def _moe_kernel(h_ref, comb_ref, wgu_ref, wd_ref, g_ref, b_ref, o_ref, acc_ref, hb_ref):
    e = pl.program_id(1)

    @pl.when(e == 0)
    def _():
        acc_ref[...] = jnp.zeros_like(acc_ref)
        hb_ref[...] = h_ref[...].astype(BF16)

    gu = jnp.dot(hb_ref[...], wgu_ref[0], preferred_element_type=F32)
    gt, up = gu[:, :D_EXPERT], gu[:, D_EXPERT:]
    comb = comb_ref[...]
    lane = lax.broadcasted_iota(I32, comb.shape, 1)
    ce = jnp.sum(jnp.where(lane == e, comb, 0.0), axis=-1, keepdims=True)
    act = (gt * _sigmoid(gt)) * up * ce
    acc_ref[...] += jnp.dot(act.astype(BF16), wd_ref[0], preferred_element_type=F32)

    @pl.when(e == N_EXPERTS - 1)
    def _():
        o_ref[...] = _layer_norm(DN_ALPHA * h_ref[...] + acc_ref[...], g_ref[...], b_ref[...])


def _moe(h2d, comb, wgu, wd, g, b, tm=1024):
    n, d = h2d.shape
    tm = min(tm, n)
    return pl.pallas_call(
        _moe_kernel, grid=(n // tm, N_EXPERTS),
        in_specs=[pl.BlockSpec((tm, d), lambda i, e: (i, 0)),
                  pl.BlockSpec((tm, N_EXPERTS), lambda i, e: (i, 0)),
                  pl.BlockSpec((1, d, 2 * D_EXPERT), lambda i, e: (e, 0, 0)),
                  pl.BlockSpec((1, D_EXPERT, d), lambda i, e: (e, 0, 0)),
                  pl.BlockSpec((1, d), lambda i, e: (0, 0)),
                  pl.BlockSpec((1, d), lambda i, e: (0, 0))],
        out_specs=pl.BlockSpec((tm, d), lambda i, e: (i, 0)),
        out_shape=jax.ShapeDtypeStruct((n, d), F32),
        scratch_shapes=[pltpu.VMEM((tm, d), F32), pltpu.VMEM((tm, d), BF16)],
        compiler_params=_cparams("parallel", "arbitrary"), name="moe",
    )(h2d, comb, wgu, wd, g.reshape(1, -1), b.reshape(1, -1))


def _rope_full(z, cos, sin_signed):
    return z * cos + pltpu.roll(z, LANE // 2, axis=1) * sin_signed


def _rope_half(z, cos, sin_signed, first_half):
    partner = jnp.where(first_half, pltpu.roll(z, LANE - C_IDX_DIM // 2, axis=1),
                        pltpu.roll(z, C_IDX_DIM // 2, axis=1))
    return z * cos + partner * sin_signed


def _dsa_prep_kernel(p_ref, c128_ref, s128_ref, c64_ref, s64_ref, lng_ref, lnb_ref,
                     q_ref, k_ref, v_ref, iq_ref, ika_ref, ikb_ref, iw_ref):
    c128, s128 = c128_ref[...], s128_ref[...]
    c64, s64 = c64_ref[...], s64_ref[...]
    lane = lax.broadcasted_iota(I32, c64.shape, 1)
    first_half = (lane & (C_IDX_DIM - 1)) < C_IDX_DIM // 2
    for h in range(C_HEADS):
        hs = slice(h * LANE, (h + 1) * LANE)
        q_ref[0, :, hs] = (_rope_full(p_ref[0, :, hs], c128, s128) * Q_SCALE).astype(BF16)
    k0 = C_WIDTH
    k_ref[0] = _rope_full(p_ref[0, :, k0:k0 + LANE], c128, s128).astype(BF16)
    v_ref[0] = p_ref[0, :, k0 + LANE:k0 + 2 * LANE].astype(BF16)
    i0 = k0 + 2 * LANE
    for j in range(C_IDX_HEADS * C_IDX_DIM // LANE):
        js = slice(j * LANE, (j + 1) * LANE)
        z = p_ref[0, :, i0 + j * LANE:i0 + (j + 1) * LANE]
        iq_ref[0, :, js] = (_rope_half(z, c64, s64, first_half) * C_IDX_DIM ** -0.5).astype(BF16)
    t0 = i0 + C_IDX_HEADS * C_IDX_DIM
    tile = p_ref[0, :, t0:t0 + LANE]
    is_key = lane < C_IDX_DIM
    mu = jnp.sum(jnp.where(is_key, tile, 0.0), axis=-1, keepdims=True) * (1.0 / C_IDX_DIM)
    zc = jnp.where(is_key, tile - mu, 0.0)
    var = jnp.sum(zc * zc, axis=-1, keepdims=True) * (1.0 / C_IDX_DIM)
    ikn = zc * lax.rsqrt(var + LN_EPS) * lng_ref[...] + lnb_ref[...]
    ikr = jnp.where(is_key, _rope_half(ikn, c64, s64, first_half), 0.0)
    ika_ref[0] = ikr.astype(BF16)
    ikb_ref[0] = pltpu.roll(ikr, C_IDX_DIM, axis=1).astype(BF16)
    iw_ref[0] = tile * C_IDX_HEADS ** -0.5


def _dsa_prep(p, tabs, lng, lnb, tm=256):
    b, t, cols = p.shape
    tab = pl.BlockSpec((tm, LANE), lambda i, j: (j, 0))
    par = pl.BlockSpec((1, LANE), lambda i, j: (0, 0))
    o = lambda w: pl.BlockSpec((1, tm, w), lambda i, j: (i, j, 0))
    widths = [C_WIDTH, LANE, LANE, C_IDX_HEADS * C_IDX_DIM, LANE, LANE, LANE]
    dtypes = [BF16, BF16, BF16, BF16, BF16, BF16, F32]
    return pl.pallas_call(
        _dsa_prep_kernel, grid=(b, t // tm),
        in_specs=[pl.BlockSpec((1, tm, cols), lambda i, j: (i, j, 0)), tab, tab, tab, tab, par, par],
        out_specs=[o(w) for w in widths],
        out_shape=[jax.ShapeDtypeStruct((b, t, w), dt) for w, dt in zip(widths, dtypes)],
        compiler_params=_cparams("parallel", "parallel"), name="dsa_prep",
    )(p, *tabs, lng, lnb)


def _fold8(z, op):
    return op(z.reshape(z.shape[0] // 8, 8, z.shape[1]), axis=0)


def _dsa_attn_kernel(q_ref, iq_ref, iw_ref, k_ref, vt_ref, ikab_ref, o_ref,
                     sc_ref, tie_ref, bias_ref, s_ref, acc_ref, *, ktop, idx_bits):
    i = pl.program_id(1)
    tq = q_ref.shape[2]
    kg = k_ref.shape[2]
    assert tq == LANE and kg % tq == 0
    ng = i // (kg // tq) + 1
    keypos0 = lax.broadcasted_iota(I32, (kg, tq), 0)
    qpos = i * tq + lax.broadcasted_iota(I32, (kg, tq), 1)

    iqt = iq_ref[0]
    rhs_i = jnp.concatenate([iqt[:LANE], iqt[LANE:]], axis=1)
    w = [iw_ref[0, h:h + 1, :] for h in range(C_IDX_HEADS)]

    def score_body(g, carry):
        sab = jnp.dot(ikab_ref[0, g], rhs_i, preferred_element_type=F32)
        relu = lambda z: jnp.maximum(z, 0.0)
        sc = (relu(sab[:kg, :tq]) * w[0] + relu(sab[kg:, :tq]) * w[1]
              + relu(sab[:kg, tq:]) * w[2] + relu(sab[kg:, tq:]) * w[3]) + 0.0
        sc_ref[g] = jnp.where(g * kg + keypos0 <= qpos, sc, -jnp.inf)
        return carry

    lax.fori_loop(0, ng, score_body, 0)

    def count(ref, pred):
        def body(g, acc):
            return acc + _fold8(jnp.where(pred(ref[g]), 1, 0), jnp.sum)
        acc = lax.fori_loop(0, ng, body, jnp.zeros((8, tq), I32))
        return jnp.sum(acc, axis=0, keepdims=True)

    def key_to_float(key):
        return pltpu.bitcast(key ^ ((key >> 31) & 0x7FFFFFFF), F32)

    def bisect(it, carry):
        tkey, c_lo = carry
        ckey = tkey + lax.shift_left(jnp.int32(1), 31 - it)
        cand = key_to_float(ckey)
        cnt = jnp.where(ckey <= KEY_NEG_INF, ng * kg, count(sc_ref, lambda x: x >= cand))
        take = cnt >= ktop
        return jnp.where(take, ckey, tkey), jnp.where(take, cnt, c_lo)

    tkey, c_lo = lax.fori_loop(0, 32, bisect, (jnp.full((1, tq), INT_MIN, I32),
                                               jnp.zeros((1, tq), I32) + ng * kg))
    tau = key_to_float(tkey)

    tie_rows = jnp.where(c_lo > ktop, jnp.where(tau >= F32_LOWEST, 1, 0), 0)
    need_tie = jnp.max(tie_rows)

    @pl.when(need_tie == 0)
    def _():
        thr = jnp.maximum(tau, F32_LOWEST)

        def body(g, carry):
            bias_ref[g] = jnp.where(sc_ref[g] >= thr, 0.0, NEG_BIG)
            return carry

        lax.fori_loop(0, ng, body, 0)

    @pl.when(need_tie > 0)
    def _():
        need = ktop - count(sc_ref, lambda x: x > tau)

        def tie_body(g, carry):
            sc = sc_ref[g]
            idx = jnp.where(sc >= F32_LOWEST, g * kg + keypos0, TIE_NONE)
            tie_ref[g] = jnp.where(sc == tau, idx, TIE_NONE)
            return carry

        lax.fori_loop(0, ng, tie_body, 0)

        def bisect_idx(it, ans):
            cand = ans + lax.shift_left(jnp.int32(1), idx_bits - 1 - it)
            cnt = count(tie_ref, lambda x: x < cand)
            return jnp.where(cnt < need, cand, ans)

        jstar = lax.fori_loop(0, idx_bits, bisect_idx, jnp.zeros((1, tq), I32))
        thr = jnp.maximum(tau, F32_LOWEST)

        def body(g, carry):
            sel = jnp.where(sc_ref[g] > thr, 1, jnp.where(tie_ref[g] <= jstar, 1, 0))
            bias_ref[g] = jnp.where(sel > 0, 0.0, NEG_BIG)
            return carry

        lax.fori_loop(0, ng, body, 0)

    qt = q_ref[0]
    pairs = range(C_HEADS // 2)
    q_cols = [jnp.concatenate([qt[(2 * j) * LANE:(2 * j + 1) * LANE],
                               qt[(2 * j + 1) * LANE:(2 * j + 2) * LANE]], axis=1) for j in pairs]

    def pass1(g, m8s):
        k_g = k_ref[0, g]
        bias2 = jnp.concatenate([bias_ref[g]] * 2, axis=1)
        out = []
        for j in pairs:
            s2 = jnp.dot(k_g, q_cols[j], preferred_element_type=F32) + bias2
            s_ref[j, g] = s2
            out.append(jnp.maximum(m8s[j], _fold8(s2, jnp.max)))
        return tuple(out)

    m8s = lax.fori_loop(0, ng, pass1, tuple(jnp.full((8, 2 * tq), NEG_BIG, F32) for _ in pairs))
    ms = [jnp.max(m8, axis=0, keepdims=True) for m8 in m8s]
    acc_ref[...] = jnp.zeros_like(acc_ref)

    def pass2(g, l8s):
        vt_g = vt_ref[0, g]
        out = []
        for j in pairs:
            p2 = jnp.exp2(s_ref[j, g] - ms[j])
            acc_ref[j] += jnp.dot(vt_g, p2.astype(BF16), preferred_element_type=F32)
            out.append(l8s[j] + _fold8(p2, jnp.sum))
        return tuple(out)

    l8s = lax.fori_loop(0, ng, pass2, tuple(jnp.zeros((8, 2 * tq), F32) for _ in pairs))
    for j in pairs:
        out2 = acc_ref[j] / jnp.sum(l8s[j], axis=0, keepdims=True)
        for u in range(2):
            h = 2 * j + u
            o_ref[0, :, h * LANE:(h + 1) * LANE] = out2[:, u * tq:(u + 1) * tq].T


def _dsa_attn(qt, iqt, iwt, kg4, vtg4, ikab4, ktop):
    b, _, t = qt.shape
    tq = LANE
    ngt, kg = kg4.shape[1], kg4.shape[2]
    qspec = lambda w: pl.BlockSpec((1, tq, w), lambda i, j: (i, j, 0))
    whole = lambda z: pl.BlockSpec((1,) + z.shape[1:], lambda i, j: (i, 0, 0, 0))
    kern = functools.partial(_dsa_attn_kernel, ktop=ktop, idx_bits=int(math.log2(t)))
    scratch = [pltpu.VMEM((ngt, kg, tq), F32), pltpu.VMEM((ngt, kg, tq), I32),
               pltpu.VMEM((ngt, kg, tq), F32), pltpu.VMEM((C_HEADS // 2, ngt, kg, 2 * tq), F32),
               pltpu.VMEM((C_HEADS // 2, C_HEAD_DIM, 2 * tq), F32)]
    return pl.pallas_call(
        kern, grid=(b, t // tq),
        in_specs=[pl.BlockSpec((1, C_WIDTH, tq), lambda i, j: (i, 0, j)),
                  pl.BlockSpec((1, C_IDX_HEADS * C_IDX_DIM, tq), lambda i, j: (i, 0, j)),
                  pl.BlockSpec((1, C_IDX_HEADS, tq), lambda i, j: (i, 0, j)),
                  whole(kg4), whole(vtg4), whole(ikab4)],
        out_specs=qspec(C_WIDTH),
        out_shape=jax.ShapeDtypeStruct((b, t, C_WIDTH), F32),
        scratch_shapes=scratch,
        compiler_params=_cparams("parallel", "arbitrary"), name="dsa_attn",
    )(qt, iqt, iwt, kg4, vtg4, ikab4)


def _rope_tables(t):
    pos = jnp.arange(t, dtype=F32)

    def tab(dim):
        half = dim // 2
        inv = ROPE_THETA ** (-jnp.arange(half, dtype=F32) / half)
        ang = pos[:, None] * inv[None, :]
        cos, sin = jnp.cos(ang), jnp.sin(ang)
        reps = LANE // dim
        return (jnp.tile(jnp.concatenate([cos, cos], axis=-1), (1, reps)),
                jnp.tile(jnp.concatenate([-sin, sin], axis=-1), (1, reps)))

    c128, s128 = tab(C_HEAD_DIM)
    c64, s64 = tab(C_IDX_DIM)
    return c128, s128, c64, s64


def _pad_cols(w, cols):
    return jnp.pad(w, ((0, 0), (0, cols - w.shape[1])))


def _moe_block(h, comb_t, w_gate, w_up, w_down, g, b):
    wgu = jnp.concatenate([w_gate, w_up], axis=-1).astype(BF16)
    return _moe(h, comb_t.T, wgu, w_down.astype(BF16), g, b)


@jax.jit
def _forward(x, w_in_even, a_mu, a_w0, a_w2, a_a0, a_a2, a_g2, a_kk_scale, a_ka_scale, a_r_k,
             a_gn_g, a_gn_b, b_gate_w2, b_gate_b, b_norm_g, w_out_even, w_in_odd, c_ik_ln_g,
             c_ik_ln_b, w_out_odd, ln1_g, ln1_b, ln2_g, ln2_b, router_w, router_bias,
             exp_w_gate, exp_w_up, exp_w_down):
    bsz, t, d = x.shape
    assert d == D_MODEL and t % 256 == 0 and (t & (t - 1)) == 0
    n = bsz * t
    ktop = min(C_INDEX_TOPK, t // 4)
    rwt = router_w.T
    xf = x.reshape(n, d)
    hsum = (jnp.arange(A_WIDTH)[:, None] // A_HEAD_DIM
            == jnp.arange(A_WIDTH)[None, :] // A_HEAD_DIM).astype(BF16)

    for l in range(DEPTH):
        i = l // 2
        if l % 2 == 0:
            w = w_in_even[i]
            pa, pb = _proj(xf, [w[:, :A_COLS].astype(BF16),
                                _pad_cols(w[:, A_COLS:], B_PAD_COLS).astype(BF16)])
            wwa = jnp.zeros((LANE, 2 * A_WIDTH), F32)
            wwa = wwa.at[:A_DECAY_LORA, :A_WIDTH].set(a_w2[i]).at[A_DECAY_LORA:, A_WIDTH:].set(a_a2[i])
            ya = _rwkv(pa.reshape(bsz, t, A_COLS), a_mu[i], wwa, a_w0[i], a_a0[i], a_g2[i],
                       a_kk_scale[i], a_ka_scale[i], hsum, a_r_k[i].reshape(-1), a_gn_g[i], a_gn_b[i])
            gw2p = jnp.zeros((LANE, B_KEY_WIDTH), F32).at[:B_GATE_LORA].set(b_gate_w2[i])
            yb = _gla(pb.reshape(bsz, t, B_PAD_COLS), gw2p, b_gate_b[i], b_norm_g[i])
            wo = w_out_even[i].astype(BF16)
            h, comb_t = _outproj_ln([ya.reshape(n, A_WIDTH), yb.reshape(n, B_VAL_WIDTH)],
                                    [wo[:A_WIDTH], wo[A_WIDTH:]], xf, ln1_g[l], ln1_b[l],
                                    rwt, router_bias)
        else:
            (p,) = _proj(xf, [_pad_cols(w_in_odd[i], ODD_PAD_COLS).astype(BF16)])
            pad = lambda z: jnp.pad(z, (0, LANE - z.shape[0])).reshape(1, LANE)
            q, k, v, iq, ika, ikb, iw = _dsa_prep(p.reshape(bsz, t, ODD_PAD_COLS), _rope_tables(t),
                                                  pad(c_ik_ln_g[i]), pad(c_ik_ln_b[i]))
            kgrp = min(KEY_GROUP, t)
            grp = lambda z: z.reshape(bsz, t // kgrp, kgrp, LANE)
            iwt = jnp.swapaxes(iw[:, :, C_IDX_DIM:C_IDX_DIM + C_IDX_HEADS], 1, 2)
            att = _dsa_attn(jnp.swapaxes(q, 1, 2), jnp.swapaxes(iq, 1, 2), iwt, grp(k),
                            jnp.swapaxes(grp(v), 2, 3),
                            jnp.concatenate([grp(ika), grp(ikb)], axis=2), ktop)
            h, comb_t = _outproj_ln([att.reshape(n, C_WIDTH)], [w_out_odd[i].astype(BF16)], xf,
                                    ln1_g[l], ln1_b[l], rwt, router_bias)
        xf = _moe_block(h, comb_t, exp_w_gate[l], exp_w_up[l], exp_w_down[l], ln2_g[l], ln2_b[l])
    return xf.reshape(bsz, t, d)


def kernel(x, w_in_even, a_mu, a_w0, a_w2, a_a0, a_a2, a_g2, a_kk_scale, a_ka_scale, a_r_k, a_gn_g, a_gn_b, b_gate_w2, b_gate_b, b_norm_g, w_out_even, w_in_odd, c_ik_ln_g, c_ik_ln_b, w_out_odd, ln1_g, ln1_b, ln2_g, ln2_b, router_w, router_bias, exp_w_gate, exp_w_up, exp_w_down):
    return _forward(x, w_in_even, a_mu, a_w0, a_w2, a_a0, a_a2, a_g2, a_kk_scale, a_ka_scale, a_r_k,
                    a_gn_g, a_gn_b, b_gate_w2, b_gate_b, b_norm_g, w_out_even, w_in_odd, c_ik_ln_g,
                    c_ik_ln_b, w_out_odd, ln1_g, ln1_b, ln2_g, ln2_b, router_w, router_bias,
                    exp_w_gate, exp_w_up, exp_w_down)
```

```python
import functools
import math

import numpy as np
import jax
import jax.numpy as jnp
from jax import lax
from jax.experimental import pallas as pl
from jax.experimental.pallas import tpu as pltpu

F32 = jnp.float32
BF16 = jnp.bfloat16
I32 = jnp.int32

D_MODEL = 1024
DEPTH = 2
A_HEADS, A_HEAD_DIM = 8, 64
A_WIDTH = A_HEADS * A_HEAD_DIM
A_DECAY_LORA, A_ICLR_LORA, A_GATE_LORA = 64, 64, 128
A_GN_EPS = 64e-5
A_COLS = 3 * A_WIDTH + A_DECAY_LORA + A_ICLR_LORA + A_GATE_LORA
B_HEADS, B_KEY_DIM, B_VAL_DIM = 4, 64, 128
B_KEY_WIDTH = B_HEADS * B_KEY_DIM
B_VAL_WIDTH = B_HEADS * B_VAL_DIM
B_GATE_LORA = 16
B_GATE_TAU = 16.0
B_NORM_EPS = 1e-5
B_COLS = 2 * B_KEY_WIDTH + 2 * B_VAL_WIDTH + B_GATE_LORA
C_HEADS, C_HEAD_DIM = 8, 128
C_WIDTH = C_HEADS * C_HEAD_DIM
C_IDX_HEADS, C_IDX_DIM = 4, 64
C_INDEX_TOPK = 256
ODD_COLS = C_WIDTH + 2 * C_HEAD_DIM + C_IDX_HEADS * C_IDX_DIM + C_IDX_DIM + C_IDX_HEADS
ROPE_THETA = 10000.0
N_EXPERTS, N_GROUPS, TOP_K, D_EXPERT = 16, 4, 2, 256
EXPERTS_PER_GROUP = N_EXPERTS // N_GROUPS
DN_ALPHA = (2 * DEPTH) ** 0.25
LN_EPS = 1e-5

LANE = 128
CHUNK = 64
INV_BLOCK = 16
VMEM_LIMIT = 56 * 1024 * 1024
B_PAD_COLS = 13 * LANE
ODD_PAD_COLS = 13 * LANE
INT_MIN = -2 ** 31
NEG_BIG = -1e30
TIE_NONE = 2 ** 30
F32_LOWEST = float(np.finfo(np.float32).min)
KEY_NEG_INF = INT_MIN + 0x7FFFFF
KEY_GROUP = 4 * LANE
Q_SCALE = math.log2(math.e) * C_HEAD_DIM ** -0.5


def _cparams(*sem):
    return pltpu.CompilerParams(dimension_semantics=sem, vmem_limit_bytes=VMEM_LIMIT)


def _dot(a, b):
    return jnp.dot(a.astype(BF16), b.astype(BF16), preferred_element_type=F32)


def _dot_nt(a, b):
    return lax.dot_general(a.astype(BF16), b.astype(BF16), (((1,), (1,)), ((), ())),
                           preferred_element_type=F32)


def _dot_tn(a, b):
    return lax.dot_general(a.astype(BF16), b.astype(BF16), (((0,), (0,)), ((), ())),
                           preferred_element_type=F32)


def _split2(a):
    hi = a.astype(BF16)
    lo = (a - hi.astype(F32)).astype(BF16)
    return hi, lo


def _split3(a):
    hi = a.astype(BF16)
    r1 = a - hi.astype(F32)
    mid = r1.astype(BF16)
    lo = (r1 - mid.astype(F32)).astype(BF16)
    return hi, mid, lo


def _dot_exact_lhs(l_bf16, a):
    hi, mid, lo = _split3(a)
    d = lambda z: jnp.dot(l_bf16, z, preferred_element_type=F32)
    return d(hi) + d(mid) + d(lo)


def _dot_exact_rhs(a, r_bf16):
    hi, mid, lo = _split3(a)
    d = lambda z: jnp.dot(z, r_bf16, preferred_element_type=F32)
    return d(hi) + d(mid) + d(lo)


def _dot3(a, b):
    ah, al = _split2(a)
    bh, bl = _split2(b)
    d = lambda x, y: jnp.dot(x, y, preferred_element_type=F32)
    return d(ah, bh) + d(ah, bl) + d(al, bh)


def _dot3_nt(a, b):
    ah, al = _split2(a)
    bh, bl = _split2(b)
    d = lambda x, y: lax.dot_general(x, y, (((1,), (1,)), ((), ())), preferred_element_type=F32)
    return d(ah, bh) + d(ah, bl) + d(al, bh)


def _sigmoid(z):
    return 1.0 / (1.0 + jnp.exp(-z))


def _layer_norm(z, g, b):
    mu = jnp.mean(z, axis=-1, keepdims=True)
    zc = z - mu
    var = jnp.mean(zc * zc, axis=-1, keepdims=True)
    return zc * lax.rsqrt(var + LN_EPS) * g + b


def _proj_kernel(x_ref, *refs):
    n_out = len(refs) // 2
    xb = x_ref[...].astype(BF16)
    for w_ref, o_ref in zip(refs[:n_out], refs[n_out:]):
        o_ref[...] = jnp.dot(xb, w_ref[...], preferred_element_type=F32)


def _proj(x2d, ws, tm=512):
    n, d = x2d.shape
    in_specs = [pl.BlockSpec((tm, d), lambda i: (i, 0))]
    in_specs += [pl.BlockSpec(w.shape, lambda i: (0, 0)) for w in ws]
    out_specs = [pl.BlockSpec((tm, w.shape[1]), lambda i: (i, 0)) for w in ws]
    out_shape = [jax.ShapeDtypeStruct((n, w.shape[1]), F32) for w in ws]
    return pl.pallas_call(
        _proj_kernel, grid=(n // tm,), in_specs=in_specs, out_specs=out_specs,
        out_shape=out_shape, compiler_params=_cparams("parallel"), name="in_proj",
    )(x2d, *ws)


def _rwkv_front(p, prev, mu, wwa, w0, a0, g2, kks_scale, kas_scale, hsum):
    row = lax.broadcasted_iota(I32, p.shape, 0)
    shifted = jnp.where(row == 0, prev, pltpu.roll(p, 1, axis=0))
    pm = p + (shifted - p) * mu

    w = A_WIDTH
    r = pm[:, 0:w]
    k = pm[:, w:2 * w]
    v = pm[:, 2 * w:3 * w]
    xwa = pm[:, 3 * w:3 * w + LANE]
    xg = pm[:, 3 * w + LANE:3 * w + 2 * LANE]

    lane = lax.broadcasted_iota(I32, xwa.shape, 1)
    lhs = jnp.where(lane < A_DECAY_LORA, jnp.tanh(xwa), xwa)
    z = _dot3(lhs, wwa)
    lw = -_sigmoid(z[:, 0:w] + w0) * math.exp(-0.5)
    a = _sigmoid(z[:, w:2 * w] + a0)
    g = _dot(_sigmoid(xg), g2)

    kks = k * kks_scale
    ss = _dot_exact_rhs(kks * kks, hsum)
    kkn = kks * lax.rsqrt(jnp.maximum(ss, 1e-24))
    kh = k * (1.0 + (a - 1.0) * kas_scale)
    return r, lw, kh, v, kkn, kkn * a, g


HEAD_GROUP = 4
GROUP_W = HEAD_GROUP * A_HEAD_DIM


def _bdx(y, bd_mask):
    return jnp.where(bd_mask, jnp.concatenate([y] * HEAD_GROUP, axis=0), 0.0).astype(BF16)


def _tri_inv_groups(mats, blk_mask, eye_l, bd_mask):
    mm = lambda x, y: jnp.dot(x.astype(BF16), _bdx(y, bd_mask), preferred_element_type=F32)
    assert CHUNK // INV_BLOCK == 4
    dg = [jnp.where(blk_mask, a, 0.0) for a in mats]
    e = [a - d for a, d in zip(mats, dg)]
    pw = [-d for d in dg]
    dinv = [eye_l + p for p in pw]
    for _ in range(int(math.log2(INV_BLOCK)) - 1):
        pw = [mm(p, p) for p in pw]
        dinv = [mm(d, eye_l + p) for d, p in zip(dinv, pw)]
    f = [mm(d, e_) for d, e_ in zip(dinv, e)]
    m = [d - mm(f_, d) for d, f_ in zip(dinv, f)]
    f2 = [mm(f_, f_) for f_ in f]
    return [m_ + mm(f2_, m_) for m_, f2_ in zip(m, f2)]


def _rwkv_core_kernel(p_ref, mu_ref, wwa_ref, w0_ref, a0_ref, g2_ref, kks_ref, kas_ref, hsum_ref,
                      rk_ref, gng_ref, gnb_ref, o_ref, s_ref, carry_ref):
    t = pl.program_id(1)

    @pl.when(t == 0)
    def _():
        s_ref[...] = jnp.zeros_like(s_ref)
        carry_ref[...] = jnp.zeros_like(carry_ref)

    p = p_ref[0]
    r_all, lw_all, k_all, v_all, kk_all, ab_all, g_all = _rwkv_front(
        p, carry_ref[...], mu_ref[...], wwa_ref[...], w0_ref[...], a0_ref[...], g2_ref[...],
        kks_ref[...], kas_ref[...], hsum_ref[...])
    carry_ref[...] = p[p.shape[0] - 1:p.shape[0], :]

    c = CHUNK
    hd = A_HEAD_DIM
    gw = GROUP_W
    n_chunk = p_ref.shape[1] // c
    groups = range(A_WIDTH // gw)
    ri = lax.broadcasted_iota(I32, (c, c), 0)
    ci = lax.broadcasted_iota(I32, (c, c), 1)
    tri_incl = jnp.where(ci <= ri, 1.0, 0.0).astype(BF16)
    lt = lax.broadcasted_iota(I32, (c, gw), 0)
    ls = lax.broadcasted_iota(I32, (c, gw), 1) & (hd - 1)
    eye_l = jnp.where(ls == lt, 1.0, 0.0).astype(F32)
    blk_mask = (lt // INV_BLOCK) == (ls // INV_BLOCK)
    gt = lax.broadcasted_iota(I32, (2 * c, gw), 0)
    gs = lax.broadcasted_iota(I32, (2 * c, gw), 1) & (hd - 1)
    g_mask = gs < (gt & (c - 1)) + jnp.where(gt >= c, 1, 0)
    bi = lax.broadcasted_iota(I32, (gw, gw), 0)
    bj = lax.broadcasted_iota(I32, (gw, gw), 1)
    bd_mask = (bi // hd) == (bj // hd)
    mm = lambda x, y: jnp.dot(x.astype(BF16), _bdx(y, bd_mask), preferred_element_type=F32)
    nt = lambda x, y: lax.dot_general(x, _bdx(y, bd_mask), (((1,), (1,)), ((), ())),
                                      preferred_element_type=F32)
    chunks = range(n_chunk)
    gsl = [slice(i * gw, (i + 1) * gw) for i in groups]
    probs = [(ic, i) for ic in chunks for i in groups]

    pre = []
    for ic in chunks:
        rows = slice(ic * c, (ic + 1) * c)
        r, lw, k, v = r_all[rows], lw_all[rows], k_all[rows], v_all[rows]
        kk, ab = kk_all[rows], ab_all[rows]
        cum = _dot_exact_lhs(tri_incl, lw)
        cl = cum[c - 1:c, :]
        e_neg = jnp.exp(-cum)
        e_end = jnp.exp(cl - cum)
        pre.append(dict(
            v=v, rq=r * jnp.exp(cum), kq=kk * jnp.exp(cum - lw),
            bd=ab * e_neg, kd=k * e_neg, be=ab * e_end, ke=k * e_end,
            p_end=jnp.exp(cl), rkk=r * k * rk_ref[...]))
    lhs = {p: jnp.concatenate([pre[p[0]]["kq"][:, gsl[p[1]]], pre[p[0]]["rq"][:, gsl[p[1]]]],
                              axis=0).astype(BF16) for p in probs}
    gb = {p: jnp.where(g_mask, nt(lhs[p], pre[p[0]]["bd"][:, gsl[p[1]]]), 0.0) for p in probs}
    gk = {p: jnp.where(g_mask, nt(lhs[p], pre[p[0]]["kd"][:, gsl[p[1]]]), 0.0) for p in probs}
    tinv = dict(zip(probs, _tri_inv_groups([gb[p][:c] for p in probs], blk_mask, eye_l, bd_mask)))
    avy = {p: mm(gk[p], pre[p[0]]["v"][:, gsl[p[1]]]) for p in probs}
    wt = {p: -mm(tinv[p], pre[p[0]]["kq"][:, gsl[p[1]]]) for p in probs}
    ut = {p: -mm(tinv[p], avy[p][:c]) for p in probs}

    for ic in chunks:
        sl = pl.ds(ic * c, c)
        d = pre[ic]
        g = g_all[ic * c:(ic + 1) * c]
        s0s = [s_ref[i] for i in groups]
        uy = [_dot_nt(jnp.concatenate([wt[ic, i], lhs[ic, i][c:]], axis=0), s0s[i]) for i in groups]
        us = [uy[i][:c] + ut[ic, i] for i in groups]
        y2 = [mm(gb[ic, i][c:], us[i]) for i in groups]
        sn = [_dot_tn(jnp.concatenate([us[i], d["v"][:, gsl[i]]], axis=0),
                      jnp.concatenate([d["be"][:, gsl[i]], d["ke"][:, gsl[i]]], axis=0)) for i in groups]
        for i in groups:
            s_ref[i] = s0s[i] * d["p_end"][:, gsl[i]] + jnp.where(bd_mask, sn[i], 0.0)
            y_g = uy[i][c:] + y2[i] + avy[ic, i][c:]
            for j in range(HEAD_GROUP):
                hs = slice(i * gw + j * hd, i * gw + (j + 1) * hd)
                y = y_g[:, j * hd:(j + 1) * hd]
                ym = jnp.mean(y, axis=-1, keepdims=True)
                yc = y - ym
                yv = jnp.mean(yc * yc, axis=-1, keepdims=True)
                yn = yc * lax.rsqrt(yv + A_GN_EPS) * gng_ref[:, hs] + gnb_ref[:, hs]
                bonus = jnp.sum(d["rkk"][:, hs], axis=-1, keepdims=True) * d["v"][:, hs]
                o_ref[0, sl, hs] = (yn + bonus) * g[:, hs]


def _rwkv(pa, mu, wwa, w0, a0, g2, kks, kas, hsum, r_k, gn_g, gn_b, tb=256):
    b, t, _ = pa.shape
    w = A_WIDTH
    row = lambda z: z.reshape(1, -1)
    full = lambda z: pl.BlockSpec(z.shape, lambda i, j: (0,) * z.ndim)
    args = [pa, row(mu), wwa, row(w0), row(a0), g2, row(kks), row(kas), hsum,
            row(r_k), row(gn_g), row(gn_b)]
    in_specs = [pl.BlockSpec((1, tb, A_COLS), lambda i, j: (i, j, 0))] + [full(z) for z in args[1:]]
    return pl.pallas_call(
        _rwkv_core_kernel, grid=(b, t // tb), in_specs=in_specs,
        out_specs=pl.BlockSpec((1, tb, w), lambda i, j: (i, j, 0)),
        out_shape=jax.ShapeDtypeStruct((b, t, w), F32),
        scratch_shapes=[pltpu.VMEM((A_WIDTH // GROUP_W, GROUP_W, GROUP_W), F32),
                        pltpu.VMEM((1, A_COLS), F32)],
        compiler_params=_cparams("parallel", "arbitrary"), name="rwkv",
    )(*args)


def _gla_kernel(p_ref, gw2_ref, gb_ref, ng_ref, o_ref, s_ref):
    t = pl.program_id(1)

    @pl.when(t == 0)
    def _():
        s_ref[...] = jnp.zeros_like(s_ref)

    c = CHUNK
    kw, vw = B_KEY_WIDTH, B_VAL_WIDTH
    n_chunk = p_ref.shape[1] // c
    ri = lax.broadcasted_iota(I32, (c, c), 0)
    ci = lax.broadcasted_iota(I32, (c, c), 1)
    causal = ci <= ri
    tri_incl = jnp.where(causal, 1.0, 0.0).astype(BF16)

    chunks = range(n_chunk)
    heads = range(B_HEADS)
    ksl = [slice(h * B_KEY_DIM, (h + 1) * B_KEY_DIM) for h in heads]
    vsl = [slice(h * B_VAL_DIM, (h + 1) * B_VAL_DIM) for h in heads]
    probs = [(ic, h) for ic in chunks for h in heads]

    pre = []
    for ic in chunks:
        sl = pl.ds(ic * c, c)
        q = p_ref[0, sl, 0:kw] * (B_KEY_DIM ** -0.5)
        k = p_ref[0, sl, kw:2 * kw]
        xa = p_ref[0, sl, 2 * kw + 2 * vw:2 * kw + 2 * vw + LANE]
        z = _dot3(xa, gw2_ref[...]) + gb_ref[...]
        log_a = (jnp.minimum(z, 0.0) - jnp.log(1.0 + jnp.exp(-jnp.abs(z)))) * (1.0 / B_GATE_TAU)
        bc = _dot_exact_lhs(tri_incl, log_a)
        bl = bc[c - 1:c, :]
        pre.append(dict(q_dec=q * jnp.exp(bc), k_inv=k * jnp.exp(-bc), k_end=k * jnp.exp(bl - bc),
                        p_end=jnp.exp(bl), v=p_ref[0, sl, 2 * kw:2 * kw + vw]))
    qd = {(ic, h): pre[ic]["q_dec"][:, ksl[h]] for ic, h in probs}
    v_h = {(ic, h): pre[ic]["v"][:, vsl[h]] for ic, h in probs}
    att = {p: jnp.where(causal, _dot_nt(qd[p], pre[p[0]]["k_inv"][:, ksl[p[1]]]), 0.0) for p in probs}
    o_intra = {p: _dot(att[p], v_h[p]) for p in probs}
    sn = {p: _dot_tn(v_h[p], pre[p[0]]["k_end"][:, ksl[p[1]]]) for p in probs}

    for ic in chunks:
        sl = pl.ds(ic * c, c)
        g = p_ref[0, sl, 2 * kw + vw:2 * kw + 2 * vw]
        s0s = [s_ref[h] for h in heads]
        o_inter = [_dot_nt(qd[ic, h], s0s[h]) for h in heads]
        for h in heads:
            s_ref[h] = s0s[h] * pre[ic]["p_end"][:, ksl[h]] + sn[ic, h]
            o = o_intra[ic, h] + o_inter[h]
            o = o * lax.rsqrt(jnp.mean(o * o, axis=-1, keepdims=True) + B_NORM_EPS)
            g_h = g[:, vsl[h]]
            o_ref[0, sl, vsl[h]] = o * ng_ref[:, vsl[h]] * (g_h * _sigmoid(g_h))


def _gla(pb, gw2p, gate_b, norm_g, tb=256):
    b, t, cols = pb.shape
    return pl.pallas_call(
        _gla_kernel, grid=(b, t // tb),
        in_specs=[pl.BlockSpec((1, tb, cols), lambda i, j: (i, j, 0)),
                  pl.BlockSpec(gw2p.shape, lambda i, j: (0, 0)),
                  pl.BlockSpec((1, B_KEY_WIDTH), lambda i, j: (0, 0)),
                  pl.BlockSpec((1, B_VAL_WIDTH), lambda i, j: (0, 0))],
        out_specs=pl.BlockSpec((1, tb, B_VAL_WIDTH), lambda i, j: (i, j, 0)),
        out_shape=jax.ShapeDtypeStruct((b, t, B_VAL_WIDTH), F32),
        scratch_shapes=[pltpu.VMEM((B_HEADS, B_VAL_DIM, B_KEY_DIM), F32)],
        compiler_params=_cparams("parallel", "arbitrary"), name="gla",
    )(pb, gw2p, gate_b.reshape(1, -1), norm_g.reshape(1, -1))


def _outproj_ln_kernel(*refs):
    n_in = (len(refs) - 7) // 2
    x_ref, g_ref, b_ref, rwt_ref, rb_ref, o_ref, comb_ref = refs[2 * n_in:]
    mix = None
    for y_ref, w_ref in zip(refs[:n_in], refs[n_in:2 * n_in]):
        d = jnp.dot(y_ref[...].astype(BF16), w_ref[...], preferred_element_type=F32)
        mix = d if mix is None else mix + d
    h = _layer_norm(DN_ALPHA * x_ref[...] + mix, g_ref[...], b_ref[...])
    o_ref[...] = h
    _route(h, rwt_ref[...], rb_ref[...], comb_ref)


def _outproj_ln(ys, ws, x2d, g, b, rwt, rbias, tm=512):
    n, d = x2d.shape
    in_specs = [pl.BlockSpec((tm, y.shape[1]), lambda i: (i, 0)) for y in ys]
    in_specs += [pl.BlockSpec(w.shape, lambda i: (0, 0)) for w in ws]
    in_specs += [pl.BlockSpec((tm, d), lambda i: (i, 0)),
                 pl.BlockSpec((1, d), lambda i: (0, 0)), pl.BlockSpec((1, d), lambda i: (0, 0)),
                 pl.BlockSpec((N_EXPERTS, d), lambda i: (0, 0)),
                 pl.BlockSpec((N_EXPERTS, 1), lambda i: (0, 0))]
    return pl.pallas_call(
        _outproj_ln_kernel, grid=(n // tm,), in_specs=in_specs,
        out_specs=[pl.BlockSpec((tm, d), lambda i: (i, 0)),
                   pl.BlockSpec((N_EXPERTS, tm), lambda i: (0, i))],
        out_shape=[jax.ShapeDtypeStruct((n, d), F32), jax.ShapeDtypeStruct((N_EXPERTS, n), F32)],
        compiler_params=_cparams("parallel"), name="out_proj_ln",
    )(*ys, *ws, x2d, g.reshape(1, -1), b.reshape(1, -1), rwt, rbias.reshape(-1, 1))


def _route(h, rwt, rb, o_ref):
    logits = _dot3_nt(rwt, h)
    s = _sigmoid(logits)
    sel = s + rb
    s_rows = [s[e:e + 1, :] for e in range(N_EXPERTS)]
    rows = [sel[e:e + 1, :] for e in range(N_EXPERTS)]
    best_val, best = None, None
    for gidx in range(N_GROUPS):
        mem = rows[gidx * EXPERTS_PER_GROUP:(gidx + 1) * EXPERTS_PER_GROUP]
        gs = None
        for i in range(EXPERTS_PER_GROUP):
            for j in range(i + 1, EXPERTS_PER_GROUP):
                pair = mem[i] + mem[j]
                gs = pair if gs is None else jnp.maximum(gs, pair)
        if best_val is None:
            best_val, best = gs, jnp.zeros(gs.shape, I32)
        else:
            upd = gs > best_val
            best = jnp.where(upd, gidx, best)
            best_val = jnp.where(upd, gs, best_val)
    vals = [jnp.where(best == (e // EXPERTS_PER_GROUP), rows[e], -jnp.inf) for e in range(N_EXPERTS)]

    def arg_top(vs):
        m = functools.reduce(jnp.maximum, vs)
        idx = jnp.full(m.shape, N_EXPERTS, I32)
        for e in reversed(range(N_EXPERTS)):
            idx = jnp.where(vs[e] == m, e, idx)
        return idx

    i1 = arg_top(vals)
    i2 = arg_top([jnp.where(i1 == e, -jnp.inf, vals[e]) for e in range(N_EXPERTS)])
    g1 = functools.reduce(jnp.add, [jnp.where(i1 == e, s_rows[e], 0.0) for e in range(N_EXPERTS)])
    g2 = functools.reduce(jnp.add, [jnp.where(i2 == e, s_rows[e], 0.0) for e in range(N_EXPERTS)])
    tot = g1 + g2
    for e in range(N_EXPERTS):
        o_ref[e:e + 1, :] = jnp.where(i1 == e, g1 / tot, 0.0) + jnp.where(i2 == e, g2 / tot, 0.0)


MOE_EXPERTS_PER_STEP = 4


def _moe_kernel(h_ref, comb_ref, wgu_ref, wd_ref, g_ref, b_ref, o_ref, acc_ref, hb_ref):
    s = pl.program_id(1)

    @pl.when(s == 0)
    def _():
        acc_ref[...] = jnp.zeros_like(acc_ref)
        hb_ref[...] = h_ref[...].astype(BF16)

    hb = hb_ref[...]
    comb = comb_ref[...]
    lane = lax.broadcasted_iota(I32, comb.shape, 1)
    acts = []
    for j in range(MOE_EXPERTS_PER_STEP):
        e = s * MOE_EXPERTS_PER_STEP + j
        gu = jnp.dot(hb, wgu_ref[j], preferred_element_type=F32)
        gt, up = gu[:, :D_EXPERT], gu[:, D_EXPERT:]
        ce = jnp.sum(jnp.where(lane == e, comb, 0.0), axis=-1, keepdims=True)
        acts.append(((gt * _sigmoid(gt)) * up * ce).astype(BF16))
    wd = wd_ref[...].reshape(MOE_EXPERTS_PER_STEP * D_EXPERT, wd_ref.shape[2])
    acc_ref[...] += jnp.dot(jnp.concatenate(acts, axis=1), wd, preferred_element_type=F32)

    @pl.when(s == pl.num_programs(1) - 1)
    def _():
        o_ref[...] = _layer_norm(DN_ALPHA * h_ref[...] + acc_ref[...], g_ref[...], b_ref[...])


def _moe(h2d, comb, wgu, wd, g, b, tm=1024):
    n, d = h2d.shape
    tm = min(tm, n)
    eps = MOE_EXPERTS_PER_STEP
    return pl.pallas_call(
        _moe_kernel, grid=(n // tm, N_EXPERTS // eps),
        in_specs=[pl.BlockSpec((tm, d), lambda i, e: (i, 0)),
                  pl.BlockSpec((tm, N_EXPERTS), lambda i, e: (i, 0)),
                  pl.BlockSpec((eps, d, 2 * D_EXPERT), lambda i, e: (e, 0, 0)),
                  pl.BlockSpec((eps, D_EXPERT, d), lambda i, e: (e, 0, 0)),
                  pl.BlockSpec((1, d), lambda i, e: (0, 0)),
                  pl.BlockSpec((1, d), lambda i, e: (0, 0))],
        out_specs=pl.BlockSpec((tm, d), lambda i, e: (i, 0)),
        out_shape=jax.ShapeDtypeStruct((n, d), F32),
        scratch_shapes=[pltpu.VMEM((tm, d), F32), pltpu.VMEM((tm, d), BF16)],
        compiler_params=_cparams("parallel", "arbitrary"), name="moe",
    )(h2d, comb, wgu, wd, g.reshape(1, -1), b.reshape(1, -1))


def _rope_full(z, cos, sin_signed):
    return z * cos + pltpu.roll(z, LANE // 2, axis=1) * sin_signed


def _rope_half(z, cos, sin_signed, first_half):
    partner = jnp.where(first_half, pltpu.roll(z, LANE - C_IDX_DIM // 2, axis=1),
                        pltpu.roll(z, C_IDX_DIM // 2, axis=1))
    return z * cos + partner * sin_signed


def _dsa_prep_kernel(p_ref, c128_ref, s128_ref, c64_ref, s64_ref, lng_ref, lnb_ref,
                     q_ref, k_ref, v_ref, iq_ref, ika_ref, ikb_ref, iw_ref):
    c128, s128 = c128_ref[...], s128_ref[...]
    c64, s64 = c64_ref[...], s64_ref[...]
    lane = lax.broadcasted_iota(I32, c64.shape, 1)
    first_half = (lane & (C_IDX_DIM - 1)) < C_IDX_DIM // 2
    for h in range(C_HEADS):
        hs = slice(h * LANE, (h + 1) * LANE)
        q_ref[0, :, hs] = (_rope_full(p_ref[0, :, hs], c128, s128) * Q_SCALE).astype(BF16)
    k0 = C_WIDTH
    k_ref[0] = _rope_full(p_ref[0, :, k0:k0 + LANE], c128, s128).astype(BF16)
    v_ref[0] = p_ref[0, :, k0 + LANE:k0 + 2 * LANE].astype(BF16)
    i0 = k0 + 2 * LANE
    for j in range(C_IDX_HEADS * C_IDX_DIM // LANE):
        js = slice(j * LANE, (j + 1) * LANE)
        z = p_ref[0, :, i0 + j * LANE:i0 + (j + 1) * LANE]
        iq_ref[0, :, js] = (_rope_half(z, c64, s64, first_half) * C_IDX_DIM ** -0.5).astype(BF16)
    t0 = i0 + C_IDX_HEADS * C_IDX_DIM
    tile = p_ref[0, :, t0:t0 + LANE]
    is_key = lane < C_IDX_DIM
    mu = jnp.sum(jnp.where(is_key, tile, 0.0), axis=-1, keepdims=True) * (1.0 / C_IDX_DIM)
    zc = jnp.where(is_key, tile - mu, 0.0)
    var = jnp.sum(zc * zc, axis=-1, keepdims=True) * (1.0 / C_IDX_DIM)
    ikn = zc * lax.rsqrt(var + LN_EPS) * lng_ref[...] + lnb_ref[...]
    ikr = jnp.where(is_key, _rope_half(ikn, c64, s64, first_half), 0.0)
    ika_ref[0] = ikr.astype(BF16)
    ikb_ref[0] = pltpu.roll(ikr, C_IDX_DIM, axis=1).astype(BF16)
    iw_ref[0] = tile * C_IDX_HEADS ** -0.5


def _dsa_prep(p, tabs, lng, lnb, tm=256):
    b, t, cols = p.shape
    tab = pl.BlockSpec((tm, LANE), lambda i, j: (j, 0))
    par = pl.BlockSpec((1, LANE), lambda i, j: (0, 0))
    o = lambda w: pl.BlockSpec((1, tm, w), lambda i, j: (i, j, 0))
    widths = [C_WIDTH, LANE, LANE, C_IDX_HEADS * C_IDX_DIM, LANE, LANE, LANE]
    dtypes = [BF16, BF16, BF16, BF16, BF16, BF16, F32]
    return pl.pallas_call(
        _dsa_prep_kernel, grid=(b, t // tm),
        in_specs=[pl.BlockSpec((1, tm, cols), lambda i, j: (i, j, 0)), tab, tab, tab, tab, par, par],
        out_specs=[o(w) for w in widths],
        out_shape=[jax.ShapeDtypeStruct((b, t, w), dt) for w, dt in zip(widths, dtypes)],
        compiler_params=_cparams("parallel", "parallel"), name="dsa_prep",
    )(p, *tabs, lng, lnb)


def _fold8(z, op):
    return op(z.reshape(z.shape[0] // 8, 8, z.shape[1]), axis=0)


def _dsa_attn_kernel(q_ref, iq_ref, iw_ref, k_ref, vt_ref, ikab_ref, o_ref,
                     sc_ref, tie_ref, bias_ref, s_ref, acc_ref, *, ktop, idx_bits):
    i = pl.program_id(1)
    tq = q_ref.shape[2]
    kg = k_ref.shape[2]
    assert tq == LANE and kg % tq == 0
    ng = i // (kg // tq) + 1
    keypos0 = lax.broadcasted_iota(I32, (kg, tq), 0)
    qpos = i * tq + lax.broadcasted_iota(I32, (kg, tq), 1)

    iqt = iq_ref[0]
    rhs_i = jnp.concatenate([iqt[:LANE], iqt[LANE:]], axis=1)
    w = [iw_ref[0, h:h + 1, :] for h in range(C_IDX_HEADS)]

    def score_body(g, carry):
        sab = jnp.dot(ikab_ref[0, g], rhs_i, preferred_element_type=F32)
        relu = lambda z: jnp.maximum(z, 0.0)
        sc = (relu(sab[:kg, :tq]) * w[0] + relu(sab[kg:, :tq]) * w[1]
              + relu(sab[:kg, tq:]) * w[2] + relu(sab[kg:, tq:]) * w[3]) + 0.0
        sc_ref[g] = jnp.where(g * kg + keypos0 <= qpos, sc, -jnp.inf)
        return carry

    lax.fori_loop(0, ng, score_body, 0)

    def count(ref, pred, n_groups=None):
        def body(g, acc):
            return acc + _fold8(jnp.where(pred(ref[g]), 1, 0), jnp.sum)
        acc = jnp.zeros((8, tq), I32)
        if n_groups is None:
            acc = lax.fori_loop(0, ng, body, acc)
        else:
            for g in range(n_groups):
                acc = body(g, acc)
        return jnp.sum(acc, axis=0, keepdims=True)

    def key_to_float(key):
        return pltpu.bitcast(key ^ ((key >> 31) & 0x7FFFFFFF), F32)

    def bisect_for(n_groups):
        def run():
            def bisect(it, carry):
                tkey, c_lo = carry
                ckey = tkey + lax.shift_left(jnp.int32(1), 31 - it)
                cand = key_to_float(ckey)
                cnt = jnp.where(ckey <= KEY_NEG_INF, n_groups * kg,
                                count(sc_ref, lambda x: x >= cand, n_groups))
                take = cnt >= ktop
                return jnp.where(take, ckey, tkey), jnp.where(take, cnt, c_lo)

            return lax.fori_loop(0, 32, bisect, (jnp.full((1, tq), INT_MIN, I32),
                                                 jnp.full((1, tq), n_groups * kg, I32)))
        return run

    tkey, c_lo = lax.switch(ng - 1, [bisect_for(n) for n in range(1, sc_ref.shape[0] + 1)])
    tau = key_to_float(tkey)

    tie_rows = jnp.where(c_lo > ktop, jnp.where(tau >= F32_LOWEST, 1, 0), 0)
    need_tie = jnp.max(tie_rows)

    @pl.when(need_tie == 0)
    def _():
        thr = jnp.maximum(tau, F32_LOWEST)

        def body(g, carry):
            bias_ref[g] = jnp.where(sc_ref[g] >= thr, 0.0, NEG_BIG)
            return carry

        lax.fori_loop(0, ng, body, 0)

    @pl.when(need_tie > 0)
    def _():
        need = ktop - count(sc_ref, lambda x: x > tau)

        def tie_body(g, carry):
            sc = sc_ref[g]
            idx = jnp.where(sc >= F32_LOWEST, g * kg + keypos0, TIE_NONE)
            tie_ref[g] = jnp.where(sc == tau, idx, TIE_NONE)
            return carry

        lax.fori_loop(0, ng, tie_body, 0)

        def bisect_idx(it, ans):
            cand = ans + lax.shift_left(jnp.int32(1), idx_bits - 1 - it)
            cnt = count(tie_ref, lambda x: x < cand)
            return jnp.where(cnt < need, cand, ans)

        jstar = lax.fori_loop(0, idx_bits, bisect_idx, jnp.zeros((1, tq), I32))
        thr = jnp.maximum(tau, F32_LOWEST)

        def body(g, carry):
            sel = jnp.where(sc_ref[g] > thr, 1, jnp.where(tie_ref[g] <= jstar, 1, 0))
            bias_ref[g] = jnp.where(sel > 0, 0.0, NEG_BIG)
            return carry

        lax.fori_loop(0, ng, body, 0)

    qt = q_ref[0]
    pairs = range(C_HEADS // 2)
    q_cols = [jnp.concatenate([qt[(2 * j) * LANE:(2 * j + 1) * LANE],
                               qt[(2 * j + 1) * LANE:(2 * j + 2) * LANE]], axis=1) for j in pairs]

    def pass1(g, m8s):
        k_g = k_ref[0, g]
        bias2 = jnp.concatenate([bias_ref[g]] * 2, axis=1)
        out = []
        for j in pairs:
            s2 = jnp.dot(k_g, q_cols[j], preferred_element_type=F32) + bias2
            s_ref[j, g] = s2
            out.append(jnp.maximum(m8s[j], _fold8(s2, jnp.max)))
        return tuple(out)

    m8s = lax.fori_loop(0, ng, pass1, tuple(jnp.full((8, 2 * tq), NEG_BIG, F32) for _ in pairs))
    ms = [jnp.max(m8, axis=0, keepdims=True) for m8 in m8s]
    acc_ref[...] = jnp.zeros_like(acc_ref)

    def pass2(g, l8s):
        vt_g = vt_ref[0, g]
        out = []
        for j in pairs:
            p2 = jnp.exp2(s_ref[j, g] - ms[j])
            acc_ref[j] += jnp.dot(vt_g, p2.astype(BF16), preferred_element_type=F32)
            out.append(l8s[j] + _fold8(p2, jnp.sum))
        return tuple(out)

    l8s = lax.fori_loop(0, ng, pass2, tuple(jnp.zeros((8, 2 * tq), F32) for _ in pairs))
    for j in pairs:
        out2 = acc_ref[j] / jnp.sum(l8s[j], axis=0, keepdims=True)
        for u in range(2):
            h = 2 * j + u
            o_ref[0, :, h * LANE:(h + 1) * LANE] = out2[:, u * tq:(u + 1) * tq].T


def _dsa_attn(qt, iqt, iwt, kg4, vtg4, ikab4, ktop):
    b, _, t = qt.shape
    tq = LANE
    ngt, kg = kg4.shape[1], kg4.shape[2]
    qspec = lambda w: pl.BlockSpec((1, tq, w), lambda i, j: (i, j, 0))
    whole = lambda z: pl.BlockSpec((1,) + z.shape[1:], lambda i, j: (i, 0, 0, 0))
    kern = functools.partial(_dsa_attn_kernel, ktop=ktop, idx_bits=int(math.log2(t)))
    scratch = [pltpu.VMEM((ngt, kg, tq), F32), pltpu.VMEM((ngt, kg, tq), I32),
               pltpu.VMEM((ngt, kg, tq), F32), pltpu.VMEM((C_HEADS // 2, ngt, kg, 2 * tq), F32),
               pltpu.VMEM((C_HEADS // 2, C_HEAD_DIM, 2 * tq), F32)]
    return pl.pallas_call(
        kern, grid=(b, t // tq),
        in_specs=[pl.BlockSpec((1, C_WIDTH, tq), lambda i, j: (i, 0, j)),
                  pl.BlockSpec((1, C_IDX_HEADS * C_IDX_DIM, tq), lambda i, j: (i, 0, j)),
                  pl.BlockSpec((1, C_IDX_HEADS, tq), lambda i, j: (i, 0, j)),
                  whole(kg4), whole(vtg4), whole(ikab4)],
        out_specs=qspec(C_WIDTH),
        out_shape=jax.ShapeDtypeStruct((b, t, C_WIDTH), F32),
        scratch_shapes=scratch,
        compiler_params=_cparams("parallel", "arbitrary"), name="dsa_attn",
    )(qt, iqt, iwt, kg4, vtg4, ikab4)


def _rope_tables(t):
    pos = jnp.arange(t, dtype=F32)

    def tab(dim):
        half = dim // 2
        inv = ROPE_THETA ** (-jnp.arange(half, dtype=F32) / half)
        ang = pos[:, None] * inv[None, :]
        cos, sin = jnp.cos(ang), jnp.sin(ang)
        reps = LANE // dim
        return (jnp.tile(jnp.concatenate([cos, cos], axis=-1), (1, reps)),
                jnp.tile(jnp.concatenate([-sin, sin], axis=-1), (1, reps)))

    c128, s128 = tab(C_HEAD_DIM)
    c64, s64 = tab(C_IDX_DIM)
    return c128, s128, c64, s64


def _pad_cols(w, cols):
    return jnp.pad(w, ((0, 0), (0, cols - w.shape[1])))


def _moe_block(h, comb_t, w_gate, w_up, w_down, g, b):
    wgu = jnp.concatenate([w_gate, w_up], axis=-1).astype(BF16)
    return _moe(h, comb_t.T, wgu, w_down.astype(BF16), g, b)


@jax.jit
def _forward(x, w_in_even, a_mu, a_w0, a_w2, a_a0, a_a2, a_g2, a_kk_scale, a_ka_scale, a_r_k,
             a_gn_g, a_gn_b, b_gate_w2, b_gate_b, b_norm_g, w_out_even, w_in_odd, c_ik_ln_g,
             c_ik_ln_b, w_out_odd, ln1_g, ln1_b, ln2_g, ln2_b, router_w, router_bias,
             exp_w_gate, exp_w_up, exp_w_down):
    bsz, t, d = x.shape
    assert d == D_MODEL and t % 256 == 0 and (t & (t - 1)) == 0
    n = bsz * t
    ktop = min(C_INDEX_TOPK, t // 4)
    rwt = router_w.T
    xf = x.reshape(n, d)
    hsum = (jnp.arange(A_WIDTH)[:, None] // A_HEAD_DIM
            == jnp.arange(A_WIDTH)[None, :] // A_HEAD_DIM).astype(BF16)

    for l in range(DEPTH):
        i = l // 2
        if l % 2 == 0:
            w = w_in_even[i]
            pa, pb = _proj(xf, [w[:, :A_COLS].astype(BF16),
                                _pad_cols(w[:, A_COLS:], B_PAD_COLS).astype(BF16)])
            wwa = jnp.zeros((LANE, 2 * A_WIDTH), F32)
            wwa = wwa.at[:A_DECAY_LORA, :A_WIDTH].set(a_w2[i]).at[A_DECAY_LORA:, A_WIDTH:].set(a_a2[i])
            ya = _rwkv(pa.reshape(bsz, t, A_COLS), a_mu[i], wwa, a_w0[i], a_a0[i], a_g2[i],
                       a_kk_scale[i], a_ka_scale[i], hsum, a_r_k[i].reshape(-1), a_gn_g[i], a_gn_b[i])
            gw2p = jnp.zeros((LANE, B_KEY_WIDTH), F32).at[:B_GATE_LORA].set(b_gate_w2[i])
            yb = _gla(pb.reshape(bsz, t, B_PAD_COLS), gw2p, b_gate_b[i], b_norm_g[i])
            wo = w_out_even[i].astype(BF16)
            h, comb_t = _outproj_ln([ya.reshape(n, A_WIDTH), yb.reshape(n, B_VAL_WIDTH)],
                                    [wo[:A_WIDTH], wo[A_WIDTH:]], xf, ln1_g[l], ln1_b[l],
                                    rwt, router_bias)
        else:
            (p,) = _proj(xf, [_pad_cols(w_in_odd[i], ODD_PAD_COLS).astype(BF16)])
            pad = lambda z: jnp.pad(z, (0, LANE - z.shape[0])).reshape(1, LANE)
            q, k, v, iq, ika, ikb, iw = _dsa_prep(p.reshape(bsz, t, ODD_PAD_COLS), _rope_tables(t),
                                                  pad(c_ik_ln_g[i]), pad(c_ik_ln_b[i]))
            kgrp = min(KEY_GROUP, t)
            grp = lambda z: z.reshape(bsz, t // kgrp, kgrp, LANE)
            iwt = jnp.swapaxes(iw[:, :, C_IDX_DIM:C_IDX_DIM + C_IDX_HEADS], 1, 2)
            att = _dsa_attn(jnp.swapaxes(q, 1, 2), jnp.swapaxes(iq, 1, 2), iwt, grp(k),
                            jnp.swapaxes(grp(v), 2, 3),
                            jnp.concatenate([grp(ika), grp(ikb)], axis=2), ktop)
            h, comb_t = _outproj_ln([att.reshape(n, C_WIDTH)], [w_out_odd[i].astype(BF16)], xf,
                                    ln1_g[l], ln1_b[l], rwt, router_bias)
        xf = _moe_block(h, comb_t, exp_w_gate[l], exp_w_up[l], exp_w_down[l], ln2_g[l], ln2_b[l])
    return xf.reshape(bsz, t, d)


def kernel(x, w_in_even, a_mu, a_w0, a_w2, a_a0, a_a2, a_g2, a_kk_scale, a_ka_scale, a_r_k, a_gn_g, a_gn_b, b_gate_w2, b_gate_b, b_norm_g, w_out_even, w_in_odd, c_ik_ln_g, c_ik_ln_b, w_out_odd, ln1_g, ln1_b, ln2_g, ln2_b, router_w, router_bias, exp_w_gate, exp_w_up, exp_w_down):
    return _forward(x, w_in_even, a_mu, a_w0, a_w2, a_a0, a_a2, a_g2, a_kk_scale, a_ka_scale, a_r_k,
                    a_gn_g, a_gn_b, b_gate_w2, b_gate_b, b_norm_g, w_out_even, w_in_odd, c_ik_ln_g,
                    c_ik_ln_b, w_out_odd, ln1_g, ln1_b, ln2_g, ln2_b, router_w, router_bias,
                    exp_w_gate, exp_w_up, exp_w_down)
```

```python
import functools
import math

import numpy as np
import jax
import jax.numpy as jnp
from jax import lax
from jax.experimental import pallas as pl
from jax.experimental.pallas import tpu as pltpu

F32 = jnp.float32
BF16 = jnp.bfloat16
I32 = jnp.int32

D_MODEL = 1024
DEPTH = 2
A_HEADS, A_HEAD_DIM = 8, 64
A_WIDTH = A_HEADS * A_HEAD_DIM
A_DECAY_LORA, A_ICLR_LORA, A_GATE_LORA = 64, 64, 128
A_GN_EPS = 64e-5
A_COLS = 3 * A_WIDTH + A_DECAY_LORA + A_ICLR_LORA + A_GATE_LORA
B_HEADS, B_KEY_DIM, B_VAL_DIM = 4, 64, 128
B_KEY_WIDTH = B_HEADS * B_KEY_DIM
B_VAL_WIDTH = B_HEADS * B_VAL_DIM
B_GATE_LORA = 16
B_GATE_TAU = 16.0
B_NORM_EPS = 1e-5
B_COLS = 2 * B_KEY_WIDTH + 2 * B_VAL_WIDTH + B_GATE_LORA
C_HEADS, C_HEAD_DIM = 8, 128
C_WIDTH = C_HEADS * C_HEAD_DIM
C_IDX_HEADS, C_IDX_DIM = 4, 64
C_INDEX_TOPK = 256
ODD_COLS = C_WIDTH + 2 * C_HEAD_DIM + C_IDX_HEADS * C_IDX_DIM + C_IDX_DIM + C_IDX_HEADS
ROPE_THETA = 10000.0
N_EXPERTS, N_GROUPS, TOP_K, D_EXPERT = 16, 4, 2, 256
EXPERTS_PER_GROUP = N_EXPERTS // N_GROUPS
DN_ALPHA = (2 * DEPTH) ** 0.25
LN_EPS = 1e-5

LANE = 128
CHUNK = 64
INV_BLOCK = 16
VMEM_LIMIT = 56 * 1024 * 1024
B_PAD_COLS = 13 * LANE
ODD_PAD_COLS = 13 * LANE
INT_MIN = -2 ** 31
NEG_BIG = -1e30
TIE_NONE = 2 ** 30
F32_LOWEST = float(np.finfo(np.float32).min)
KEY_NEG_INF = INT_MIN + 0x7FFFFF
KEY_GROUP = 4 * LANE
Q_SCALE = math.log2(math.e) * C_HEAD_DIM ** -0.5


def _cparams(*sem):
    return pltpu.CompilerParams(dimension_semantics=sem, vmem_limit_bytes=VMEM_LIMIT)


def _dot(a, b):
    return jnp.dot(a.astype(BF16), b.astype(BF16), preferred_element_type=F32)


def _dot_nt(a, b):
    return lax.dot_general(a.astype(BF16), b.astype(BF16), (((1,), (1,)), ((), ())),
                           preferred_element_type=F32)


def _dot_tn(a, b):
    return lax.dot_general(a.astype(BF16), b.astype(BF16), (((0,), (0,)), ((), ())),
                           preferred_element_type=F32)


def _split2(a):
    hi = a.astype(BF16)
    lo = (a - hi.astype(F32)).astype(BF16)
    return hi, lo


def _split3(a):
    hi = a.astype(BF16)
    r1 = a - hi.astype(F32)
    mid = r1.astype(BF16)
    lo = (r1 - mid.astype(F32)).astype(BF16)
    return hi, mid, lo


def _dot_exact_lhs(l_bf16, a):
    hi, mid, lo = _split3(a)
    d = lambda z: jnp.dot(l_bf16, z, preferred_element_type=F32)
    return d(hi) + d(mid) + d(lo)


def _dot_exact_rhs(a, r_bf16):
    hi, mid, lo = _split3(a)
    d = lambda z: jnp.dot(z, r_bf16, preferred_element_type=F32)
    return d(hi) + d(mid) + d(lo)


def _dot3(a, b):
    ah, al = _split2(a)
    bh, bl = _split2(b)
    d = lambda x, y: jnp.dot(x, y, preferred_element_type=F32)
    return d(ah, bh) + d(ah, bl) + d(al, bh)


def _dot3_nt(a, b):
    ah, al = _split2(a)
    bh, bl = _split2(b)
    d = lambda x, y: lax.dot_general(x, y, (((1,), (1,)), ((), ())), preferred_element_type=F32)
    return d(ah, bh) + d(ah, bl) + d(al, bh)


def _sigmoid(z):
    return 1.0 / (1.0 + jnp.exp(-z))


def _layer_norm(z, g, b):
    mu = jnp.mean(z, axis=-1, keepdims=True)
    zc = z - mu
    var = jnp.mean(zc * zc, axis=-1, keepdims=True)
    return zc * lax.rsqrt(var + LN_EPS) * g + b


def _proj_kernel(x_ref, *refs):
    n_out = len(refs) // 2
    xb = x_ref[...].astype(BF16)
    for w_ref, o_ref in zip(refs[:n_out], refs[n_out:]):
        o_ref[...] = jnp.dot(xb, w_ref[...], preferred_element_type=F32)


def _proj(x2d, ws, tm=512):
    n, d = x2d.shape
    in_specs = [pl.BlockSpec((tm, d), lambda i: (i, 0))]
    in_specs += [pl.BlockSpec(w.shape, lambda i: (0, 0)) for w in ws]
    out_specs = [pl.BlockSpec((tm, w.shape[1]), lambda i: (i, 0)) for w in ws]
    out_shape = [jax.ShapeDtypeStruct((n, w.shape[1]), F32) for w in ws]
    return pl.pallas_call(
        _proj_kernel, grid=(n // tm,), in_specs=in_specs, out_specs=out_specs,
        out_shape=out_shape, compiler_params=_cparams("parallel"), name="in_proj",
    )(x2d, *ws)


def _rwkv_front(p, prev, mu, wwa, w0, a0, g2, kks_scale, kas_scale, hsum):
    row = lax.broadcasted_iota(I32, p.shape, 0)
    shifted = jnp.where(row == 0, prev, pltpu.roll(p, 1, axis=0))
    pm = p + (shifted - p) * mu

    w = A_WIDTH
    r = pm[:, 0:w]
    k = pm[:, w:2 * w]
    v = pm[:, 2 * w:3 * w]
    xwa = pm[:, 3 * w:3 * w + LANE]
    xg = pm[:, 3 * w + LANE:3 * w + 2 * LANE]

    lane = lax.broadcasted_iota(I32, xwa.shape, 1)
    lhs = jnp.where(lane < A_DECAY_LORA, jnp.tanh(xwa), xwa)
    z = _dot3(lhs, wwa)
    lw = -_sigmoid(z[:, 0:w] + w0) * math.exp(-0.5)
    a = _sigmoid(z[:, w:2 * w] + a0)
    g = _dot(_sigmoid(xg), g2)

    kks = k * kks_scale
    ss = _dot_exact_rhs(kks * kks, hsum)
    kkn = kks * lax.rsqrt(jnp.maximum(ss, 1e-24))
    kh = k * (1.0 + (a - 1.0) * kas_scale)
    return r, lw, kh, v, kkn, kkn * a, g


HEAD_GROUP = 4
GROUP_W = HEAD_GROUP * A_HEAD_DIM


def _bdx(y, bd_mask):
    return jnp.where(bd_mask, jnp.concatenate([y] * HEAD_GROUP, axis=0), 0.0).astype(BF16)


def _tri_inv_groups(mats, blk_mask, eye_l, bd_mask):
    mm = lambda x, y: jnp.dot(x.astype(BF16), _bdx(y, bd_mask), preferred_element_type=F32)
    assert CHUNK // INV_BLOCK == 4
    dg = [jnp.where(blk_mask, a, 0.0) for a in mats]
    e = [a - d for a, d in zip(mats, dg)]
    pw = [-d for d in dg]
    dinv = [eye_l + p for p in pw]
    for _ in range(int(math.log2(INV_BLOCK)) - 1):
        pw = [mm(p, p) for p in pw]
        dinv = [mm(d, eye_l + p) for d, p in zip(dinv, pw)]
    f = [mm(d, e_) for d, e_ in zip(dinv, e)]
    m = [d - mm(f_, d) for d, f_ in zip(dinv, f)]
    f2 = [mm(f_, f_) for f_ in f]
    return [m_ + mm(f2_, m_) for m_, f2_ in zip(m, f2)]


def _rwkv_core_kernel(p_ref, mu_ref, wwa_ref, w0_ref, a0_ref, g2_ref, kks_ref, kas_ref, hsum_ref,
                      rk_ref, gng_ref, gnb_ref, o_ref, s_ref, carry_ref):
    t = pl.program_id(1)

    @pl.when(t == 0)
    def _():
        s_ref[...] = jnp.zeros_like(s_ref)
        carry_ref[...] = jnp.zeros_like(carry_ref)

    p = p_ref[0]
    r_all, lw_all, k_all, v_all, kk_all, ab_all, g_all = _rwkv_front(
        p, carry_ref[...], mu_ref[...], wwa_ref[...], w0_ref[...], a0_ref[...], g2_ref[...],
        kks_ref[...], kas_ref[...], hsum_ref[...])
    carry_ref[...] = p[p.shape[0] - 1:p.shape[0], :]

    c = CHUNK
    hd = A_HEAD_DIM
    gw = GROUP_W
    n_chunk = p_ref.shape[1] // c
    groups = range(A_WIDTH // gw)
    ri = lax.broadcasted_iota(I32, (c, c), 0)
    ci = lax.broadcasted_iota(I32, (c, c), 1)
    tri_incl = jnp.where(ci <= ri, 1.0, 0.0).astype(BF16)
    lt = lax.broadcasted_iota(I32, (c, gw), 0)
    ls = lax.broadcasted_iota(I32, (c, gw), 1) & (hd - 1)
    eye_l = jnp.where(ls == lt, 1.0, 0.0).astype(F32)
    blk_mask = (lt // INV_BLOCK) == (ls // INV_BLOCK)
    gt = lax.broadcasted_iota(I32, (2 * c, gw), 0)
    gs = lax.broadcasted_iota(I32, (2 * c, gw), 1) & (hd - 1)
    g_mask = gs < (gt & (c - 1)) + jnp.where(gt >= c, 1, 0)
    bi = lax.broadcasted_iota(I32, (gw, gw), 0)
    bj = lax.broadcasted_iota(I32, (gw, gw), 1)
    bd_mask = (bi // hd) == (bj // hd)
    mm = lambda x, y: jnp.dot(x.astype(BF16), _bdx(y, bd_mask), preferred_element_type=F32)
    nt = lambda x, y: lax.dot_general(x, _bdx(y, bd_mask), (((1,), (1,)), ((), ())),
                                      preferred_element_type=F32)
    chunks = range(n_chunk)
    gsl = [slice(i * gw, (i + 1) * gw) for i in groups]
    probs = [(ic, i) for ic in chunks for i in groups]

    pre = []
    for ic in chunks:
        rows = slice(ic * c, (ic + 1) * c)
        r, lw, k, v = r_all[rows], lw_all[rows], k_all[rows], v_all[rows]
        kk, ab = kk_all[rows], ab_all[rows]
        cum = _dot_exact_lhs(tri_incl, lw)
        cl = cum[c - 1:c, :]
        e_neg = jnp.exp(-cum)
        e_end = jnp.exp(cl - cum)
        pre.append(dict(
            v=v, rq=r * jnp.exp(cum), kq=kk * jnp.exp(cum - lw),
            bd=ab * e_neg, kd=k * e_neg, be=ab * e_end, ke=k * e_end,
            p_end=jnp.exp(cl), rkk=r * k * rk_ref[...]))
    lhs = {p: jnp.concatenate([pre[p[0]]["kq"][:, gsl[p[1]]], pre[p[0]]["rq"][:, gsl[p[1]]]],
                              axis=0).astype(BF16) for p in probs}
    gb = {p: jnp.where(g_mask, nt(lhs[p], pre[p[0]]["bd"][:, gsl[p[1]]]), 0.0) for p in probs}
    gk = {p: jnp.where(g_mask, nt(lhs[p], pre[p[0]]["kd"][:, gsl[p[1]]]), 0.0) for p in probs}
    tinv = dict(zip(probs, _tri_inv_groups([gb[p][:c] for p in probs], blk_mask, eye_l, bd_mask)))
    avy = {p: mm(gk[p], pre[p[0]]["v"][:, gsl[p[1]]]) for p in probs}
    wt = {p: -mm(tinv[p], pre[p[0]]["kq"][:, gsl[p[1]]]) for p in probs}
    ut = {p: -mm(tinv[p], avy[p][:c]) for p in probs}

    for ic in chunks:
        sl = pl.ds(ic * c, c)
        d = pre[ic]
        g = g_all[ic * c:(ic + 1) * c]
        s0s = [s_ref[i] for i in groups]
        uy = [_dot_nt(jnp.concatenate([wt[ic, i], lhs[ic, i][c:]], axis=0), s0s[i]) for i in groups]
        us = [uy[i][:c] + ut[ic, i] for i in groups]
        y2 = [mm(gb[ic, i][c:], us[i]) for i in groups]
        sn = [_dot_tn(jnp.concatenate([us[i], d["v"][:, gsl[i]]], axis=0),
                      jnp.concatenate([d["be"][:, gsl[i]], d["ke"][:, gsl[i]]], axis=0)) for i in groups]
        for i in groups:
            s_ref[i] = s0s[i] * d["p_end"][:, gsl[i]] + jnp.where(bd_mask, sn[i], 0.0)
            y_g = uy[i][c:] + y2[i] + avy[ic, i][c:]
            for j in range(HEAD_GROUP):
                hs = slice(i * gw + j * hd, i * gw + (j + 1) * hd)
                y = y_g[:, j * hd:(j + 1) * hd]
                ym = jnp.mean(y, axis=-1, keepdims=True)
                yc = y - ym
                yv = jnp.mean(yc * yc, axis=-1, keepdims=True)
                yn = yc * lax.rsqrt(yv + A_GN_EPS) * gng_ref[:, hs] + gnb_ref[:, hs]
                bonus = jnp.sum(d["rkk"][:, hs], axis=-1, keepdims=True) * d["v"][:, hs]
                o_ref[0, sl, hs] = (yn + bonus) * g[:, hs]


def _rwkv(pa, mu, wwa, w0, a0, g2, kks, kas, hsum, r_k, gn_g, gn_b, tb=256):
    b, t, _ = pa.shape
    w = A_WIDTH
    row = lambda z: z.reshape(1, -1)
    full = lambda z: pl.BlockSpec(z.shape, lambda i, j: (0,) * z.ndim)
    args = [pa, row(mu), wwa, row(w0), row(a0), g2, row(kks), row(kas), hsum,
            row(r_k), row(gn_g), row(gn_b)]
    in_specs = [pl.BlockSpec((1, tb, A_COLS), lambda i, j: (i, j, 0))] + [full(z) for z in args[1:]]
    return pl.pallas_call(
        _rwkv_core_kernel, grid=(b, t // tb), in_specs=in_specs,
        out_specs=pl.BlockSpec((1, tb, w), lambda i, j: (i, j, 0)),
        out_shape=jax.ShapeDtypeStruct((b, t, w), F32),
        scratch_shapes=[pltpu.VMEM((A_WIDTH // GROUP_W, GROUP_W, GROUP_W), F32),
                        pltpu.VMEM((1, A_COLS), F32)],
        compiler_params=_cparams("parallel", "arbitrary"), name="rwkv",
    )(*args)


def _gla_kernel(p_ref, gw2_ref, gb_ref, ng_ref, o_ref, s_ref):
    t = pl.program_id(1)

    @pl.when(t == 0)
    def _():
        s_ref[...] = jnp.zeros_like(s_ref)

    c = CHUNK
    kw, vw = B_KEY_WIDTH, B_VAL_WIDTH
    n_chunk = p_ref.shape[1] // c
    ri = lax.broadcasted_iota(I32, (c, c), 0)
    ci = lax.broadcasted_iota(I32, (c, c), 1)
    causal = ci <= ri
    tri_incl = jnp.where(causal, 1.0, 0.0).astype(BF16)

    chunks = range(n_chunk)
    heads = range(B_HEADS)
    ksl = [slice(h * B_KEY_DIM, (h + 1) * B_KEY_DIM) for h in heads]
    vsl = [slice(h * B_VAL_DIM, (h + 1) * B_VAL_DIM) for h in heads]
    probs = [(ic, h) for ic in chunks for h in heads]

    pre = []
    for ic in chunks:
        sl = pl.ds(ic * c, c)
        q = p_ref[0, sl, 0:kw] * (B_KEY_DIM ** -0.5)
        k = p_ref[0, sl, kw:2 * kw]
        xa = p_ref[0, sl, 2 * kw + 2 * vw:2 * kw + 2 * vw + LANE]
        z = _dot3(xa, gw2_ref[...]) + gb_ref[...]
        log_a = (jnp.minimum(z, 0.0) - jnp.log(1.0 + jnp.exp(-jnp.abs(z)))) * (1.0 / B_GATE_TAU)
        bc = _dot_exact_lhs(tri_incl, log_a)
        bl = bc[c - 1:c, :]
        pre.append(dict(q_dec=q * jnp.exp(bc), k_inv=k * jnp.exp(-bc), k_end=k * jnp.exp(bl - bc),
                        p_end=jnp.exp(bl), v=p_ref[0, sl, 2 * kw:2 * kw + vw]))
    qd = {(ic, h): pre[ic]["q_dec"][:, ksl[h]] for ic, h in probs}
    v_h = {(ic, h): pre[ic]["v"][:, vsl[h]] for ic, h in probs}
    att = {p: jnp.where(causal, _dot_nt(qd[p], pre[p[0]]["k_inv"][:, ksl[p[1]]]), 0.0) for p in probs}
    o_intra = {p: _dot(att[p], v_h[p]) for p in probs}
    sn = {p: _dot_tn(v_h[p], pre[p[0]]["k_end"][:, ksl[p[1]]]) for p in probs}

    for ic in chunks:
        sl = pl.ds(ic * c, c)
        g = p_ref[0, sl, 2 * kw + vw:2 * kw + 2 * vw]
        s0s = [s_ref[h] for h in heads]
        o_inter = [_dot_nt(qd[ic, h], s0s[h]) for h in heads]
        for h in heads:
            s_ref[h] = s0s[h] * pre[ic]["p_end"][:, ksl[h]] + sn[ic, h]
            o = o_intra[ic, h] + o_inter[h]
            o = o * lax.rsqrt(jnp.mean(o * o, axis=-1, keepdims=True) + B_NORM_EPS)
            g_h = g[:, vsl[h]]
            o_ref[0, sl, vsl[h]] = o * ng_ref[:, vsl[h]] * (g_h * _sigmoid(g_h))


def _gla(pb, gw2p, gate_b, norm_g, tb=256):
    b, t, cols = pb.shape
    return pl.pallas_call(
        _gla_kernel, grid=(b, t // tb),
        in_specs=[pl.BlockSpec((1, tb, cols), lambda i, j: (i, j, 0)),
                  pl.BlockSpec(gw2p.shape, lambda i, j: (0, 0)),
                  pl.BlockSpec((1, B_KEY_WIDTH), lambda i, j: (0, 0)),
                  pl.BlockSpec((1, B_VAL_WIDTH), lambda i, j: (0, 0))],
        out_specs=pl.BlockSpec((1, tb, B_VAL_WIDTH), lambda i, j: (i, j, 0)),
        out_shape=jax.ShapeDtypeStruct((b, t, B_VAL_WIDTH), F32),
        scratch_shapes=[pltpu.VMEM((B_HEADS, B_VAL_DIM, B_KEY_DIM), F32)],
        compiler_params=_cparams("parallel", "arbitrary"), name="gla",
    )(pb, gw2p, gate_b.reshape(1, -1), norm_g.reshape(1, -1))


def _outproj_ln_kernel(*refs):
    n_in = (len(refs) - 7) // 2
    x_ref, g_ref, b_ref, rwt_ref, rb_ref, o_ref, comb_ref = refs[2 * n_in:]
    mix = None
    for y_ref, w_ref in zip(refs[:n_in], refs[n_in:2 * n_in]):
        d = jnp.dot(y_ref[...].astype(BF16), w_ref[...], preferred_element_type=F32)
        mix = d if mix is None else mix + d
    h = _layer_norm(DN_ALPHA * x_ref[...] + mix, g_ref[...], b_ref[...])
    o_ref[...] = h
    _route(h, rwt_ref[...], rb_ref[...], comb_ref)


def _outproj_ln(ys, ws, x2d, g, b, rwt, rbias, tm=512):
    n, d = x2d.shape
    in_specs = [pl.BlockSpec((tm, y.shape[1]), lambda i: (i, 0)) for y in ys]
    in_specs += [pl.BlockSpec(w.shape, lambda i: (0, 0)) for w in ws]
    in_specs += [pl.BlockSpec((tm, d), lambda i: (i, 0)),
                 pl.BlockSpec((1, d), lambda i: (0, 0)), pl.BlockSpec((1, d), lambda i: (0, 0)),
                 pl.BlockSpec((N_EXPERTS, d), lambda i: (0, 0)),
                 pl.BlockSpec((N_EXPERTS, 1), lambda i: (0, 0))]
    return pl.pallas_call(
        _outproj_ln_kernel, grid=(n // tm,), in_specs=in_specs,
        out_specs=[pl.BlockSpec((tm, d), lambda i: (i, 0)),
                   pl.BlockSpec((N_EXPERTS, tm), lambda i: (0, i))],
        out_shape=[jax.ShapeDtypeStruct((n, d), F32), jax.ShapeDtypeStruct((N_EXPERTS, n), F32)],
        compiler_params=_cparams("parallel"), name="out_proj_ln",
    )(*ys, *ws, x2d, g.reshape(1, -1), b.reshape(1, -1), rwt, rbias.reshape(-1, 1))


def _route(h, rwt, rb, o_ref):
    logits = _dot3_nt(rwt, h)
    s = _sigmoid(logits)
    sel = s + rb
    s_rows = [s[e:e + 1, :] for e in range(N_EXPERTS)]
    rows = [sel[e:e + 1, :] for e in range(N_EXPERTS)]
    best_val, best = None, None
    for gidx in range(N_GROUPS):
        mem = rows[gidx * EXPERTS_PER_GROUP:(gidx + 1) * EXPERTS_PER_GROUP]
        gs = None
        for i in range(EXPERTS_PER_GROUP):
            for j in range(i + 1, EXPERTS_PER_GROUP):
                pair = mem[i] + mem[j]
                gs = pair if gs is None else jnp.maximum(gs, pair)
        if best_val is None:
            best_val, best = gs, jnp.zeros(gs.shape, I32)
        else:
            upd = gs > best_val
            best = jnp.where(upd, gidx, best)
            best_val = jnp.where(upd, gs, best_val)
    vals = [jnp.where(best == (e // EXPERTS_PER_GROUP), rows[e], -jnp.inf) for e in range(N_EXPERTS)]

    def arg_top(vs):
        m = functools.reduce(jnp.maximum, vs)
        idx = jnp.full(m.shape, N_EXPERTS, I32)
        for e in reversed(range(N_EXPERTS)):
            idx = jnp.where(vs[e] == m, e, idx)
        return idx

    i1 = arg_top(vals)
    i2 = arg_top([jnp.where(i1 == e, -jnp.inf, vals[e]) for e in range(N_EXPERTS)])
    g1 = functools.reduce(jnp.add, [jnp.where(i1 == e, s_rows[e], 0.0) for e in range(N_EXPERTS)])
    g2 = functools.reduce(jnp.add, [jnp.where(i2 == e, s_rows[e], 0.0) for e in range(N_EXPERTS)])
    tot = g1 + g2
    for e in range(N_EXPERTS):
        o_ref[e:e + 1, :] = jnp.where(i1 == e, g1 / tot, 0.0) + jnp.where(i2 == e, g2 / tot, 0.0)


MOE_EXPERTS_PER_STEP = 4


def _moe_kernel(h_ref, comb_ref, wgu_ref, wd_ref, g_ref, b_ref, o_ref, acc_ref, hb_ref):
    s = pl.program_id(1)

    @pl.when(s == 0)
    def _():
        acc_ref[...] = jnp.zeros_like(acc_ref)
        hb_ref[...] = h_ref[...].astype(BF16)

    hb = hb_ref[...]
    comb = comb_ref[...]
    lane = lax.broadcasted_iota(I32, comb.shape, 1)
    acts = []
    for j in range(MOE_EXPERTS_PER_STEP):
        e = s * MOE_EXPERTS_PER_STEP + j
        gu = jnp.dot(hb, wgu_ref[j], preferred_element_type=F32)
        gt, up = gu[:, :D_EXPERT], gu[:, D_EXPERT:]
        ce = jnp.sum(jnp.where(lane == e, comb, 0.0), axis=-1, keepdims=True)
        acts.append(((gt * _sigmoid(gt)) * up * ce).astype(BF16))
    wd = wd_ref[...].reshape(MOE_EXPERTS_PER_STEP * D_EXPERT, wd_ref.shape[2])
    acc_ref[...] += jnp.dot(jnp.concatenate(acts, axis=1), wd, preferred_element_type=F32)

    @pl.when(s == pl.num_programs(1) - 1)
    def _():
        o_ref[...] = _layer_norm(DN_ALPHA * h_ref[...] + acc_ref[...], g_ref[...], b_ref[...])


def _moe(h2d, comb, wgu, wd, g, b, tm=1024):
    n, d = h2d.shape
    tm = min(tm, n)
    eps = MOE_EXPERTS_PER_STEP
    return pl.pallas_call(
        _moe_kernel, grid=(n // tm, N_EXPERTS // eps),
        in_specs=[pl.BlockSpec((tm, d), lambda i, e: (i, 0)),
                  pl.BlockSpec((tm, N_EXPERTS), lambda i, e: (i, 0)),
                  pl.BlockSpec((eps, d, 2 * D_EXPERT), lambda i, e: (e, 0, 0)),
                  pl.BlockSpec((eps, D_EXPERT, d), lambda i, e: (e, 0, 0)),
                  pl.BlockSpec((1, d), lambda i, e: (0, 0)),
                  pl.BlockSpec((1, d), lambda i, e: (0, 0))],
        out_specs=pl.BlockSpec((tm, d), lambda i, e: (i, 0)),
        out_shape=jax.ShapeDtypeStruct((n, d), F32),
        scratch_shapes=[pltpu.VMEM((tm, d), F32), pltpu.VMEM((tm, d), BF16)],
        compiler_params=_cparams("parallel", "arbitrary"), name="moe",
    )(h2d, comb, wgu, wd, g.reshape(1, -1), b.reshape(1, -1))


def _rope_full(z, cos, sin_signed):
    return z * cos + pltpu.roll(z, LANE // 2, axis=1) * sin_signed


def _rope_half(z, cos, sin_signed, first_half):
    partner = jnp.where(first_half, pltpu.roll(z, LANE - C_IDX_DIM // 2, axis=1),
                        pltpu.roll(z, C_IDX_DIM // 2, axis=1))
    return z * cos + partner * sin_signed


def _dsa_prep_kernel(x_ref, w_ref, c128_ref, s128_ref, c64_ref, s64_ref, lng_ref, lnb_ref,
                     q_ref, k_ref, v_ref, iq_ref, ika_ref, ikb_ref, iw_ref):
    p = jnp.dot(x_ref[0].astype(BF16), w_ref[...], preferred_element_type=F32)
    c128, s128 = c128_ref[...], s128_ref[...]
    c64, s64 = c64_ref[...], s64_ref[...]
    lane = lax.broadcasted_iota(I32, c64.shape, 1)
    first_half = (lane & (C_IDX_DIM - 1)) < C_IDX_DIM // 2
    for h in range(C_HEADS):
        hs = slice(h * LANE, (h + 1) * LANE)
        q_ref[0, :, hs] = (_rope_full(p[:, hs], c128, s128) * Q_SCALE).astype(BF16)
    k0 = C_WIDTH
    k_ref[0] = _rope_full(p[:, k0:k0 + LANE], c128, s128).astype(BF16)
    v_ref[0] = p[:, k0 + LANE:k0 + 2 * LANE].astype(BF16)
    i0 = k0 + 2 * LANE
    for j in range(C_IDX_HEADS * C_IDX_DIM // LANE):
        js = slice(j * LANE, (j + 1) * LANE)
        z = p[:, i0 + j * LANE:i0 + (j + 1) * LANE]
        iq_ref[0, :, js] = (_rope_half(z, c64, s64, first_half) * C_IDX_DIM ** -0.5).astype(BF16)
    t0 = i0 + C_IDX_HEADS * C_IDX_DIM
    tile = p[:, t0:t0 + LANE]
    is_key = lane < C_IDX_DIM
    mu = jnp.sum(jnp.where(is_key, tile, 0.0), axis=-1, keepdims=True) * (1.0 / C_IDX_DIM)
    zc = jnp.where(is_key, tile - mu, 0.0)
    var = jnp.sum(zc * zc, axis=-1, keepdims=True) * (1.0 / C_IDX_DIM)
    ikn = zc * lax.rsqrt(var + LN_EPS) * lng_ref[...] + lnb_ref[...]
    ikr = jnp.where(is_key, _rope_half(ikn, c64, s64, first_half), 0.0)
    ika_ref[0] = ikr.astype(BF16)
    ikb_ref[0] = pltpu.roll(ikr, C_IDX_DIM, axis=1).astype(BF16)
    iw_ref[0] = tile * C_IDX_HEADS ** -0.5


def _dsa_prep(x3, w, tabs, lng, lnb, tm=512):
    b, t, d = x3.shape
    tm = min(tm, t)
    tab = pl.BlockSpec((tm, LANE), lambda i, j: (j, 0))
    par = pl.BlockSpec((1, LANE), lambda i, j: (0, 0))
    o = lambda wd: pl.BlockSpec((1, tm, wd), lambda i, j: (i, j, 0))
    widths = [C_WIDTH, LANE, LANE, C_IDX_HEADS * C_IDX_DIM, LANE, LANE, LANE]
    dtypes = [BF16, BF16, BF16, BF16, BF16, BF16, F32]
    return pl.pallas_call(
        _dsa_prep_kernel, grid=(b, t // tm),
        in_specs=[pl.BlockSpec((1, tm, d), lambda i, j: (i, j, 0)),
                  pl.BlockSpec(w.shape, lambda i, j: (0, 0)), tab, tab, tab, tab, par, par],
        out_specs=[o(wd) for wd in widths],
        out_shape=[jax.ShapeDtypeStruct((b, t, wd), dt) for wd, dt in zip(widths, dtypes)],
        compiler_params=_cparams("parallel", "parallel"), name="dsa_proj_prep",
    )(x3, w, *tabs, lng, lnb)


def _fold8(z, op):
    return op(z.reshape(z.shape[0] // 8, 8, z.shape[1]), axis=0)


def _dsa_attn_kernel(q_ref, iq_ref, iw_ref, k_ref, vt_ref, ikab_ref, o_ref,
                     sc_ref, tie_ref, bias_ref, s_ref, acc_ref, *, ktop, idx_bits):
    i = pl.program_id(1)
    tq = q_ref.shape[2]
    kg = k_ref.shape[2]
    assert tq == LANE and kg % tq == 0
    ng = i // (kg // tq) + 1
    keypos0 = lax.broadcasted_iota(I32, (kg, tq), 0)
    qpos = i * tq + lax.broadcasted_iota(I32, (kg, tq), 1)

    iqt = iq_ref[0]
    rhs_i = jnp.concatenate([iqt[:LANE], iqt[LANE:]], axis=1)
    w = [iw_ref[0, h:h + 1, :] for h in range(C_IDX_HEADS)]

    def score_body(g, carry):
        sab = jnp.dot(ikab_ref[0, g], rhs_i, preferred_element_type=F32)
        relu = lambda z: jnp.maximum(z, 0.0)
        sc = (relu(sab[:kg, :tq]) * w[0] + relu(sab[kg:, :tq]) * w[1]
              + relu(sab[:kg, tq:]) * w[2] + relu(sab[kg:, tq:]) * w[3]) + 0.0
        sc_ref[g] = jnp.where(g * kg + keypos0 <= qpos, sc, -jnp.inf)
        return carry

    lax.fori_loop(0, ng, score_body, 0)

    def count(ref, pred, n_groups=None):
        def body(g, acc):
            return acc + _fold8(jnp.where(pred(ref[g]), 1, 0), jnp.sum)
        acc = jnp.zeros((8, tq), I32)
        if n_groups is None:
            acc = lax.fori_loop(0, ng, body, acc)
        else:
            for g in range(n_groups):
                acc = body(g, acc)
        return jnp.sum(acc, axis=0, keepdims=True)

    def key_to_float(key):
        return pltpu.bitcast(key ^ ((key >> 31) & 0x7FFFFFFF), F32)

    def bisect_for(n_groups):
        def run():
            def bisect(it, carry):
                tkey, c_lo = carry
                ckey = tkey + lax.shift_left(jnp.int32(1), 31 - it)
                cand = key_to_float(ckey)
                cnt = jnp.where(ckey <= KEY_NEG_INF, n_groups * kg,
                                count(sc_ref, lambda x: x >= cand, n_groups))
                take = cnt >= ktop
                return jnp.where(take, ckey, tkey), jnp.where(take, cnt, c_lo)

            return lax.fori_loop(0, 32, bisect, (jnp.full((1, tq), INT_MIN, I32),
                                                 jnp.full((1, tq), n_groups * kg, I32)))
        return run

    tkey, c_lo = lax.switch(ng - 1, [bisect_for(n) for n in range(1, sc_ref.shape[0] + 1)])
    tau = key_to_float(tkey)

    tie_rows = jnp.where(c_lo > ktop, jnp.where(tau >= F32_LOWEST, 1, 0), 0)
    need_tie = jnp.max(tie_rows)

    @pl.when(need_tie == 0)
    def _():
        thr = jnp.maximum(tau, F32_LOWEST)

        def body(g, carry):
            bias_ref[g] = jnp.where(sc_ref[g] >= thr, 0.0, NEG_BIG)
            return carry

        lax.fori_loop(0, ng, body, 0)

    @pl.when(need_tie > 0)
    def _():
        need = ktop - count(sc_ref, lambda x: x > tau)

        def tie_body(g, carry):
            sc = sc_ref[g]
            idx = jnp.where(sc >= F32_LOWEST, g * kg + keypos0, TIE_NONE)
            tie_ref[g] = jnp.where(sc == tau, idx, TIE_NONE)
            return carry

        lax.fori_loop(0, ng, tie_body, 0)

        def bisect_idx(it, ans):
            cand = ans + lax.shift_left(jnp.int32(1), idx_bits - 1 - it)
            cnt = count(tie_ref, lambda x: x < cand)
            return jnp.where(cnt < need, cand, ans)

        jstar = lax.fori_loop(0, idx_bits, bisect_idx, jnp.zeros((1, tq), I32))
        thr = jnp.maximum(tau, F32_LOWEST)

        def body(g, carry):
            sel = jnp.where(sc_ref[g] > thr, 1, jnp.where(tie_ref[g] <= jstar, 1, 0))
            bias_ref[g] = jnp.where(sel > 0, 0.0, NEG_BIG)
            return carry

        lax.fori_loop(0, ng, body, 0)

    qt = q_ref[0]
    pairs = range(C_HEADS // 2)
    q_cols = [jnp.concatenate([qt[(2 * j) * LANE:(2 * j + 1) * LANE],
                               qt[(2 * j + 1) * LANE:(2 * j + 2) * LANE]], axis=1) for j in pairs]

    def pass1(g, m8s):
        k_g = k_ref[0, g]
        bias2 = jnp.concatenate([bias_ref[g]] * 2, axis=1)
        out = []
        for j in pairs:
            s2 = jnp.dot(k_g, q_cols[j], preferred_element_type=F32) + bias2
            s_ref[j, g] = s2
            out.append(jnp.maximum(m8s[j], _fold8(s2, jnp.max)))
        return tuple(out)

    m8s = lax.fori_loop(0, ng, pass1, tuple(jnp.full((8, 2 * tq), NEG_BIG, F32) for _ in pairs))
    ms = [jnp.max(m8, axis=0, keepdims=True) for m8 in m8s]
    acc_ref[...] = jnp.zeros_like(acc_ref)

    def pass2(g, l8s):
        vt_g = vt_ref[0, g]
        out = []
        for j in pairs:
            p2 = jnp.exp2(s_ref[j, g] - ms[j])
            acc_ref[j] += jnp.dot(vt_g, p2.astype(BF16), preferred_element_type=F32)
            out.append(l8s[j] + _fold8(p2, jnp.sum))
        return tuple(out)

    l8s = lax.fori_loop(0, ng, pass2, tuple(jnp.zeros((8, 2 * tq), F32) for _ in pairs))
    for j in pairs:
        out2 = acc_ref[j] / jnp.sum(l8s[j], axis=0, keepdims=True)
        for u in range(2):
            h = 2 * j + u
            o_ref[0, :, h * LANE:(h + 1) * LANE] = out2[:, u * tq:(u + 1) * tq].T


def _dsa_attn(qt, iqt, iwt, kg4, vtg4, ikab4, ktop):
    b, _, t = qt.shape
    tq = LANE
    ngt, kg = kg4.shape[1], kg4.shape[2]
    qspec = lambda w: pl.BlockSpec((1, tq, w), lambda i, j: (i, j, 0))
    whole = lambda z: pl.BlockSpec((1,) + z.shape[1:], lambda i, j: (i, 0, 0, 0))
    kern = functools.partial(_dsa_attn_kernel, ktop=ktop, idx_bits=int(math.log2(t)))
    scratch = [pltpu.VMEM((ngt, kg, tq), F32), pltpu.VMEM((ngt, kg, tq), I32),
               pltpu.VMEM((ngt, kg, tq), F32), pltpu.VMEM((C_HEADS // 2, ngt, kg, 2 * tq), F32),
               pltpu.VMEM((C_HEADS // 2, C_HEAD_DIM, 2 * tq), F32)]
    return pl.pallas_call(
        kern, grid=(b, t // tq),
        in_specs=[pl.BlockSpec((1, C_WIDTH, tq), lambda i, j: (i, 0, j)),
                  pl.BlockSpec((1, C_IDX_HEADS * C_IDX_DIM, tq), lambda i, j: (i, 0, j)),
                  pl.BlockSpec((1, C_IDX_HEADS, tq), lambda i, j: (i, 0, j)),
                  whole(kg4), whole(vtg4), whole(ikab4)],
        out_specs=qspec(C_WIDTH),
        out_shape=jax.ShapeDtypeStruct((b, t, C_WIDTH), F32),
        scratch_shapes=scratch,
        compiler_params=_cparams("parallel", "arbitrary"), name="dsa_attn",
    )(qt, iqt, iwt, kg4, vtg4, ikab4)


def _rope_tables(t):
    pos = jnp.arange(t, dtype=F32)

    def tab(dim):
        half = dim // 2
        inv = ROPE_THETA ** (-jnp.arange(half, dtype=F32) / half)
        ang = pos[:, None] * inv[None, :]
        cos, sin = jnp.cos(ang), jnp.sin(ang)
        reps = LANE // dim
        return (jnp.tile(jnp.concatenate([cos, cos], axis=-1), (1, reps)),
                jnp.tile(jnp.concatenate([-sin, sin], axis=-1), (1, reps)))

    c128, s128 = tab(C_HEAD_DIM)
    c64, s64 = tab(C_IDX_DIM)
    return c128, s128, c64, s64


def _pad_cols(w, cols):
    return jnp.pad(w, ((0, 0), (0, cols - w.shape[1])))


def _moe_block(h, comb_t, w_gate, w_up, w_down, g, b):
    wgu = jnp.concatenate([w_gate, w_up], axis=-1).astype(BF16)
    return _moe(h, comb_t.T, wgu, w_down.astype(BF16), g, b)


@jax.jit
def _forward(x, w_in_even, a_mu, a_w0, a_w2, a_a0, a_a2, a_g2, a_kk_scale, a_ka_scale, a_r_k,
             a_gn_g, a_gn_b, b_gate_w2, b_gate_b, b_norm_g, w_out_even, w_in_odd, c_ik_ln_g,
             c_ik_ln_b, w_out_odd, ln1_g, ln1_b, ln2_g, ln2_b, router_w, router_bias,
             exp_w_gate, exp_w_up, exp_w_down):
    bsz, t, d = x.shape
    assert d == D_MODEL and t % 256 == 0 and (t & (t - 1)) == 0
    n = bsz * t
    ktop = min(C_INDEX_TOPK, t // 4)
    rwt = router_w.T
    xf = x.reshape(n, d)
    hsum = (jnp.arange(A_WIDTH)[:, None] // A_HEAD_DIM
            == jnp.arange(A_WIDTH)[None, :] // A_HEAD_DIM).astype(BF16)

    for l in range(DEPTH):
        i = l // 2
        if l % 2 == 0:
            w = w_in_even[i]
            pa, pb = _proj(xf, [w[:, :A_COLS].astype(BF16),
                                _pad_cols(w[:, A_COLS:], B_PAD_COLS).astype(BF16)])
            wwa = jnp.zeros((LANE, 2 * A_WIDTH), F32)
            wwa = wwa.at[:A_DECAY_LORA, :A_WIDTH].set(a_w2[i]).at[A_DECAY_LORA:, A_WIDTH:].set(a_a2[i])
            ya = _rwkv(pa.reshape(bsz, t, A_COLS), a_mu[i], wwa, a_w0[i], a_a0[i], a_g2[i],
                       a_kk_scale[i], a_ka_scale[i], hsum, a_r_k[i].reshape(-1), a_gn_g[i], a_gn_b[i])
            gw2p = jnp.zeros((LANE, B_KEY_WIDTH), F32).at[:B_GATE_LORA].set(b_gate_w2[i])
            yb = _gla(pb.reshape(bsz, t, B_PAD_COLS), gw2p, b_gate_b[i], b_norm_g[i])
            wo = w_out_even[i].astype(BF16)
            h, comb_t = _outproj_ln([ya.reshape(n, A_WIDTH), yb.reshape(n, B_VAL_WIDTH)],
                                    [wo[:A_WIDTH], wo[A_WIDTH:]], xf, ln1_g[l], ln1_b[l],
                                    rwt, router_bias)
        else:
            pad = lambda z: jnp.pad(z, (0, LANE - z.shape[0])).reshape(1, LANE)
            q, k, v, iq, ika, ikb, iw = _dsa_prep(
                xf.reshape(bsz, t, d), _pad_cols(w_in_odd[i], ODD_PAD_COLS).astype(BF16),
                _rope_tables(t), pad(c_ik_ln_g[i]), pad(c_ik_ln_b[i]))
            kgrp = min(KEY_GROUP, t)
            grp = lambda z: z.reshape(bsz, t // kgrp, kgrp, LANE)
            iwt = jnp.swapaxes(iw[:, :, C_IDX_DIM:C_IDX_DIM + C_IDX_HEADS], 1, 2)
            att = _dsa_attn(jnp.swapaxes(q, 1, 2), jnp.swapaxes(iq, 1, 2), iwt, grp(k),
                            jnp.swapaxes(grp(v), 2, 3),
                            jnp.concatenate([grp(ika), grp(ikb)], axis=2), ktop)
            h, comb_t = _outproj_ln([att.reshape(n, C_WIDTH)], [w_out_odd[i].astype(BF16)], xf,
                                    ln1_g[l], ln1_b[l], rwt, router_bias)
        xf = _moe_block(h, comb_t, exp_w_gate[l], exp_w_up[l], exp_w_down[l], ln2_g[l], ln2_b[l])
    return xf.reshape(bsz, t, d)


def kernel(x, w_in_even, a_mu, a_w0, a_w2, a_a0, a_a2, a_g2, a_kk_scale, a_ka_scale, a_r_k, a_gn_g, a_gn_b, b_gate_w2, b_gate_b, b_norm_g, w_out_even, w_in_odd, c_ik_ln_g, c_ik_ln_b, w_out_odd, ln1_g, ln1_b, ln2_g, ln2_b, router_w, router_bias, exp_w_gate, exp_w_up, exp_w_down):
    return _forward(x, w_in_even, a_mu, a_w0, a_w2, a_a0, a_a2, a_g2, a_kk_scale, a_ka_scale, a_r_k,
                    a_gn_g, a_gn_b, b_gate_w2, b_gate_b, b_norm_g, w_out_even, w_in_odd, c_ik_ln_g,
                    c_ik_ln_b, w_out_odd, ln1_g, ln1_b, ln2_g, ln2_b, router_w, router_bias,
                    exp_w_gate, exp_w_up, exp_w_down)
```

```python
import functools
import math

import numpy as np
import jax
import jax.numpy as jnp
from jax import lax
from jax.experimental import pallas as pl
from jax.experimental.pallas import tpu as pltpu

F32 = jnp.float32
BF16 = jnp.bfloat16
I32 = jnp.int32

D_MODEL = 1024
DEPTH = 2
A_HEADS, A_HEAD_DIM = 8, 64
A_WIDTH = A_HEADS * A_HEAD_DIM
A_DECAY_LORA, A_ICLR_LORA, A_GATE_LORA = 64, 64, 128
A_GN_EPS = 64e-5
A_COLS = 3 * A_WIDTH + A_DECAY_LORA + A_ICLR_LORA + A_GATE_LORA
B_HEADS, B_KEY_DIM, B_VAL_DIM = 4, 64, 128
B_KEY_WIDTH = B_HEADS * B_KEY_DIM
B_VAL_WIDTH = B_HEADS * B_VAL_DIM
B_GATE_LORA = 16
B_GATE_TAU = 16.0
B_NORM_EPS = 1e-5
B_COLS = 2 * B_KEY_WIDTH + 2 * B_VAL_WIDTH + B_GATE_LORA
C_HEADS, C_HEAD_DIM = 8, 128
C_WIDTH = C_HEADS * C_HEAD_DIM
C_IDX_HEADS, C_IDX_DIM = 4, 64
C_INDEX_TOPK = 256
ODD_COLS = C_WIDTH + 2 * C_HEAD_DIM + C_IDX_HEADS * C_IDX_DIM + C_IDX_DIM + C_IDX_HEADS
ROPE_THETA = 10000.0
N_EXPERTS, N_GROUPS, TOP_K, D_EXPERT = 16, 4, 2, 256
EXPERTS_PER_GROUP = N_EXPERTS // N_GROUPS
DN_ALPHA = (2 * DEPTH) ** 0.25
LN_EPS = 1e-5

LANE = 128
CHUNK = 64
INV_BLOCK = 16
VMEM_LIMIT = 56 * 1024 * 1024
B_PAD_COLS = 13 * LANE
ODD_PAD_COLS = 13 * LANE
INT_MIN = -2 ** 31
NEG_BIG = -1e30
TIE_NONE = 2 ** 30
F32_LOWEST = float(np.finfo(np.float32).min)
KEY_NEG_INF = INT_MIN + 0x7FFFFF
KEY_GROUP = 4 * LANE
Q_SCALE = math.log2(math.e) * C_HEAD_DIM ** -0.5


def _cparams(*sem):
    return pltpu.CompilerParams(dimension_semantics=sem, vmem_limit_bytes=VMEM_LIMIT)


def _dot(a, b):
    return jnp.dot(a.astype(BF16), b.astype(BF16), preferred_element_type=F32)


def _dot_nt(a, b):
    return lax.dot_general(a.astype(BF16), b.astype(BF16), (((1,), (1,)), ((), ())),
                           preferred_element_type=F32)


def _dot_tn(a, b):
    return lax.dot_general(a.astype(BF16), b.astype(BF16), (((0,), (0,)), ((), ())),
                           preferred_element_type=F32)


def _split2(a):
    hi = a.astype(BF16)
    lo = (a - hi.astype(F32)).astype(BF16)
    return hi, lo


def _split3(a):
    hi = a.astype(BF16)
    r1 = a - hi.astype(F32)
    mid = r1.astype(BF16)
    lo = (r1 - mid.astype(F32)).astype(BF16)
    return hi, mid, lo


def _dot_exact_lhs(l_bf16, a):
    hi, mid, lo = _split3(a)
    d = lambda z: jnp.dot(l_bf16, z, preferred_element_type=F32)
    return d(hi) + d(mid) + d(lo)


def _dot_exact_rhs(a, r_bf16):
    hi, mid, lo = _split3(a)
    d = lambda z: jnp.dot(z, r_bf16, preferred_element_type=F32)
    return d(hi) + d(mid) + d(lo)


def _dot3(a, b):
    ah, al = _split2(a)
    bh, bl = _split2(b)
    d = lambda x, y: jnp.dot(x, y, preferred_element_type=F32)
    return d(ah, bh) + d(ah, bl) + d(al, bh)


def _dot3_nt(a, b):
    ah, al = _split2(a)
    bh, bl = _split2(b)
    d = lambda x, y: lax.dot_general(x, y, (((1,), (1,)), ((), ())), preferred_element_type=F32)
    return d(ah, bh) + d(ah, bl) + d(al, bh)


def _sigmoid(z):
    return 1.0 / (1.0 + jnp.exp(-z))


def _layer_norm(z, g, b):
    mu = jnp.mean(z, axis=-1, keepdims=True)
    zc = z - mu
    var = jnp.mean(zc * zc, axis=-1, keepdims=True)
    return zc * lax.rsqrt(var + LN_EPS) * g + b


def _proj_kernel(x_ref, *refs):
    n_out = len(refs) // 2
    xb = x_ref[...].astype(BF16)
    for w_ref, o_ref in zip(refs[:n_out], refs[n_out:]):
        o_ref[...] = jnp.dot(xb, w_ref[...], preferred_element_type=F32)


def _proj(x2d, ws, tm=512):
    n, d = x2d.shape
    in_specs = [pl.BlockSpec((tm, d), lambda i: (i, 0))]
    in_specs += [pl.BlockSpec(w.shape, lambda i: (0, 0)) for w in ws]
    out_specs = [pl.BlockSpec((tm, w.shape[1]), lambda i: (i, 0)) for w in ws]
    out_shape = [jax.ShapeDtypeStruct((n, w.shape[1]), F32) for w in ws]
    return pl.pallas_call(
        _proj_kernel, grid=(n // tm,), in_specs=in_specs, out_specs=out_specs,
        out_shape=out_shape, compiler_params=_cparams("parallel"), name="in_proj",
    )(x2d, *ws)


def _rwkv_front(p, prev, mu, wwa, w0, a0, g2, kks_scale, kas_scale, hsum):
    row = lax.broadcasted_iota(I32, p.shape, 0)
    shifted = jnp.where(row == 0, prev, pltpu.roll(p, 1, axis=0))
    pm = p + (shifted - p) * mu

    w = A_WIDTH
    r = pm[:, 0:w]
    k = pm[:, w:2 * w]
    v = pm[:, 2 * w:3 * w]
    xwa = pm[:, 3 * w:3 * w + LANE]
    xg = pm[:, 3 * w + LANE:3 * w + 2 * LANE]

    lane = lax.broadcasted_iota(I32, xwa.shape, 1)
    lhs = jnp.where(lane < A_DECAY_LORA, jnp.tanh(xwa), xwa)
    z = _dot3(lhs, wwa)
    lw = -_sigmoid(z[:, 0:w] + w0) * math.exp(-0.5)
    a = _sigmoid(z[:, w:2 * w] + a0)
    g = _dot(_sigmoid(xg), g2)

    kks = k * kks_scale
    ss = _dot_exact_rhs(kks * kks, hsum)
    kkn = kks * lax.rsqrt(jnp.maximum(ss, 1e-24))
    kh = k * (1.0 + (a - 1.0) * kas_scale)
    return r, lw, kh, v, kkn, kkn * a, g


HEAD_GROUP = 4
GROUP_W = HEAD_GROUP * A_HEAD_DIM


def _bdx(y, bd_mask):
    return jnp.where(bd_mask, jnp.concatenate([y] * HEAD_GROUP, axis=0), 0.0).astype(BF16)


def _tri_inv_groups(mats, blk_mask, eye_l, bd_mask):
    mm = lambda x, y: jnp.dot(x.astype(BF16), _bdx(y, bd_mask), preferred_element_type=F32)
    assert CHUNK // INV_BLOCK == 4
    dg = [jnp.where(blk_mask, a, 0.0) for a in mats]
    e = [a - d for a, d in zip(mats, dg)]
    pw = [-d for d in dg]
    dinv = [eye_l + p for p in pw]
    for _ in range(int(math.log2(INV_BLOCK)) - 1):
        pw = [mm(p, p) for p in pw]
        dinv = [mm(d, eye_l + p) for d, p in zip(dinv, pw)]
    f = [mm(d, e_) for d, e_ in zip(dinv, e)]
    m = [d - mm(f_, d) for d, f_ in zip(dinv, f)]
    f2 = [mm(f_, f_) for f_ in f]
    return [m_ + mm(f2_, m_) for m_, f2_ in zip(m, f2)]


def _rwkv_core_kernel(p_ref, mu_ref, wwa_ref, w0_ref, a0_ref, g2_ref, kks_ref, kas_ref, hsum_ref,
                      rk_ref, gng_ref, gnb_ref, o_ref, s_ref, carry_ref):
    t = pl.program_id(1)

    @pl.when(t == 0)
    def _():
        s_ref[...] = jnp.zeros_like(s_ref)
        carry_ref[...] = jnp.zeros_like(carry_ref)

    p = p_ref[0]
    r_all, lw_all, k_all, v_all, kk_all, ab_all, g_all = _rwkv_front(
        p, carry_ref[...], mu_ref[...], wwa_ref[...], w0_ref[...], a0_ref[...], g2_ref[...],
        kks_ref[...], kas_ref[...], hsum_ref[...])
    carry_ref[...] = p[p.shape[0] - 1:p.shape[0], :]

    c = CHUNK
    hd = A_HEAD_DIM
    gw = GROUP_W
    n_chunk = p_ref.shape[1] // c
    groups = range(A_WIDTH // gw)
    ri = lax.broadcasted_iota(I32, (c, c), 0)
    ci = lax.broadcasted_iota(I32, (c, c), 1)
    tri_incl = jnp.where(ci <= ri, 1.0, 0.0).astype(BF16)
    lt = lax.broadcasted_iota(I32, (c, gw), 0)
    ls = lax.broadcasted_iota(I32, (c, gw), 1) & (hd - 1)
    eye_l = jnp.where(ls == lt, 1.0, 0.0).astype(F32)
    blk_mask = (lt // INV_BLOCK) == (ls // INV_BLOCK)
    gt = lax.broadcasted_iota(I32, (2 * c, gw), 0)
    gs = lax.broadcasted_iota(I32, (2 * c, gw), 1) & (hd - 1)
    g_mask = gs < (gt & (c - 1)) + jnp.where(gt >= c, 1, 0)
    bi = lax.broadcasted_iota(I32, (gw, gw), 0)
    bj = lax.broadcasted_iota(I32, (gw, gw), 1)
    bd_mask = (bi // hd) == (bj // hd)
    mm = lambda x, y: jnp.dot(x.astype(BF16), _bdx(y, bd_mask), preferred_element_type=F32)
    nt = lambda x, y: lax.dot_general(x, _bdx(y, bd_mask), (((1,), (1,)), ((), ())),
                                      preferred_element_type=F32)
    chunks = range(n_chunk)
    gsl = [slice(i * gw, (i + 1) * gw) for i in groups]
    probs = [(ic, i) for ic in chunks for i in groups]

    pre = []
    for ic in chunks:
        rows = slice(ic * c, (ic + 1) * c)
        r, lw, k, v = r_all[rows], lw_all[rows], k_all[rows], v_all[rows]
        kk, ab = kk_all[rows], ab_all[rows]
        cum = _dot_exact_lhs(tri_incl, lw)
        cl = cum[c - 1:c, :]
        e_neg = jnp.exp(-cum)
        e_end = jnp.exp(cl - cum)
        pre.append(dict(
            v=v, rq=r * jnp.exp(cum), kq=kk * jnp.exp(cum - lw),
            bd=ab * e_neg, kd=k * e_neg, be=ab * e_end, ke=k * e_end,
            p_end=jnp.exp(cl), rkk=r * k * rk_ref[...]))
    lhs = {p: jnp.concatenate([pre[p[0]]["kq"][:, gsl[p[1]]], pre[p[0]]["rq"][:, gsl[p[1]]]],
                              axis=0).astype(BF16) for p in probs}
    gb = {p: jnp.where(g_mask, nt(lhs[p], pre[p[0]]["bd"][:, gsl[p[1]]]), 0.0) for p in probs}
    gk = {p: jnp.where(g_mask, nt(lhs[p], pre[p[0]]["kd"][:, gsl[p[1]]]), 0.0) for p in probs}
    tinv = dict(zip(probs, _tri_inv_groups([gb[p][:c] for p in probs], blk_mask, eye_l, bd_mask)))
    avy = {p: mm(gk[p], pre[p[0]]["v"][:, gsl[p[1]]]) for p in probs}
    wt = {p: -mm(tinv[p], pre[p[0]]["kq"][:, gsl[p[1]]]) for p in probs}
    ut = {p: -mm(tinv[p], avy[p][:c]) for p in probs}

    for ic in chunks:
        sl = pl.ds(ic * c, c)
        d = pre[ic]
        g = g_all[ic * c:(ic + 1) * c]
        s0s = [s_ref[i] for i in groups]
        uy = [_dot_nt(jnp.concatenate([wt[ic, i], lhs[ic, i][c:]], axis=0), s0s[i]) for i in groups]
        us = [uy[i][:c] + ut[ic, i] for i in groups]
        y2 = [mm(gb[ic, i][c:], us[i]) for i in groups]
        sn = [_dot_tn(jnp.concatenate([us[i], d["v"][:, gsl[i]]], axis=0),
                      jnp.concatenate([d["be"][:, gsl[i]], d["ke"][:, gsl[i]]], axis=0)) for i in groups]
        for i in groups:
            s_ref[i] = s0s[i] * d["p_end"][:, gsl[i]] + jnp.where(bd_mask, sn[i], 0.0)
            y_g = uy[i][c:] + y2[i] + avy[ic, i][c:]
            for j in range(HEAD_GROUP):
                hs = slice(i * gw + j * hd, i * gw + (j + 1) * hd)
                y = y_g[:, j * hd:(j + 1) * hd]
                ym = jnp.mean(y, axis=-1, keepdims=True)
                yc = y - ym
                yv = jnp.mean(yc * yc, axis=-1, keepdims=True)
                yn = yc * lax.rsqrt(yv + A_GN_EPS) * gng_ref[:, hs] + gnb_ref[:, hs]
                bonus = jnp.sum(d["rkk"][:, hs], axis=-1, keepdims=True) * d["v"][:, hs]
                o_ref[0, sl, hs] = ((yn + bonus) * g[:, hs]).astype(o_ref.dtype)


def _rwkv(pa, mu, wwa, w0, a0, g2, kks, kas, hsum, r_k, gn_g, gn_b, tb=256):
    b, t, _ = pa.shape
    w = A_WIDTH
    row = lambda z: z.reshape(1, -1)
    full = lambda z: pl.BlockSpec(z.shape, lambda i, j: (0,) * z.ndim)
    args = [pa, row(mu), wwa, row(w0), row(a0), g2, row(kks), row(kas), hsum,
            row(r_k), row(gn_g), row(gn_b)]
    in_specs = [pl.BlockSpec((1, tb, A_COLS), lambda i, j: (i, j, 0))] + [full(z) for z in args[1:]]
    return pl.pallas_call(
        _rwkv_core_kernel, grid=(b, t // tb), in_specs=in_specs,
        out_specs=pl.BlockSpec((1, tb, w), lambda i, j: (i, j, 0)),
        out_shape=jax.ShapeDtypeStruct((b, t, w), BF16),
        scratch_shapes=[pltpu.VMEM((A_WIDTH // GROUP_W, GROUP_W, GROUP_W), F32),
                        pltpu.VMEM((1, A_COLS), F32)],
        compiler_params=_cparams("parallel", "arbitrary"), name="rwkv",
    )(*args)


def _gla_kernel(p_ref, gw2_ref, gb_ref, ng_ref, o_ref, s_ref):
    t = pl.program_id(1)

    @pl.when(t == 0)
    def _():
        s_ref[...] = jnp.zeros_like(s_ref)

    c = CHUNK
    kw, vw = B_KEY_WIDTH, B_VAL_WIDTH
    n_chunk = p_ref.shape[1] // c
    ri = lax.broadcasted_iota(I32, (c, c), 0)
    ci = lax.broadcasted_iota(I32, (c, c), 1)
    causal = ci <= ri
    tri_incl = jnp.where(causal, 1.0, 0.0).astype(BF16)

    chunks = range(n_chunk)
    heads = range(B_HEADS)
    ksl = [slice(h * B_KEY_DIM, (h + 1) * B_KEY_DIM) for h in heads]
    vsl = [slice(h * B_VAL_DIM, (h + 1) * B_VAL_DIM) for h in heads]
    probs = [(ic, h) for ic in chunks for h in heads]

    pre = []
    for ic in chunks:
        sl = pl.ds(ic * c, c)
        q = p_ref[0, sl, 0:kw] * (B_KEY_DIM ** -0.5)
        k = p_ref[0, sl, kw:2 * kw]
        xa = p_ref[0, sl, 2 * kw + 2 * vw:2 * kw + 2 * vw + LANE]
        z = _dot3(xa, gw2_ref[...]) + gb_ref[...]
        log_a = (jnp.minimum(z, 0.0) - jnp.log(1.0 + jnp.exp(-jnp.abs(z)))) * (1.0 / B_GATE_TAU)
        bc = _dot_exact_lhs(tri_incl, log_a)
        bl = bc[c - 1:c, :]
        pre.append(dict(q_dec=q * jnp.exp(bc), k_inv=k * jnp.exp(-bc), k_end=k * jnp.exp(bl - bc),
                        p_end=jnp.exp(bl), v=p_ref[0, sl, 2 * kw:2 * kw + vw]))
    qd = {(ic, h): pre[ic]["q_dec"][:, ksl[h]] for ic, h in probs}
    v_h = {(ic, h): pre[ic]["v"][:, vsl[h]] for ic, h in probs}
    att = {p: jnp.where(causal, _dot_nt(qd[p], pre[p[0]]["k_inv"][:, ksl[p[1]]]), 0.0) for p in probs}
    o_intra = {p: _dot(att[p], v_h[p]) for p in probs}
    sn = {p: _dot_tn(v_h[p], pre[p[0]]["k_end"][:, ksl[p[1]]]) for p in probs}

    for ic in chunks:
        sl = pl.ds(ic * c, c)
        g = p_ref[0, sl, 2 * kw + vw:2 * kw + 2 * vw]
        s0s = [s_ref[h] for h in heads]
        o_inter = [_dot_nt(qd[ic, h], s0s[h]) for h in heads]
        for h in heads:
            s_ref[h] = s0s[h] * pre[ic]["p_end"][:, ksl[h]] + sn[ic, h]
            o = o_intra[ic, h] + o_inter[h]
            o = o * lax.rsqrt(jnp.mean(o * o, axis=-1, keepdims=True) + B_NORM_EPS)
            g_h = g[:, vsl[h]]
            o_ref[0, sl, vsl[h]] = (o * ng_ref[:, vsl[h]] * (g_h * _sigmoid(g_h))).astype(o_ref.dtype)


def _gla(pb, gw2p, gate_b, norm_g, tb=256):
    b, t, cols = pb.shape
    return pl.pallas_call(
        _gla_kernel, grid=(b, t // tb),
        in_specs=[pl.BlockSpec((1, tb, cols), lambda i, j: (i, j, 0)),
                  pl.BlockSpec(gw2p.shape, lambda i, j: (0, 0)),
                  pl.BlockSpec((1, B_KEY_WIDTH), lambda i, j: (0, 0)),
                  pl.BlockSpec((1, B_VAL_WIDTH), lambda i, j: (0, 0))],
        out_specs=pl.BlockSpec((1, tb, B_VAL_WIDTH), lambda i, j: (i, j, 0)),
        out_shape=jax.ShapeDtypeStruct((b, t, B_VAL_WIDTH), BF16),
        scratch_shapes=[pltpu.VMEM((B_HEADS, B_VAL_DIM, B_KEY_DIM), F32)],
        compiler_params=_cparams("parallel", "arbitrary"), name="gla",
    )(pb, gw2p, gate_b.reshape(1, -1), norm_g.reshape(1, -1))


def _outproj_ln_kernel(*refs):
    n_in = (len(refs) - 7) // 2
    x_ref, g_ref, b_ref, rwt_ref, rb_ref, o_ref, comb_ref = refs[2 * n_in:]
    mix = None
    for y_ref, w_ref in zip(refs[:n_in], refs[n_in:2 * n_in]):
        d = jnp.dot(y_ref[...].astype(BF16), w_ref[...], preferred_element_type=F32)
        mix = d if mix is None else mix + d
    h = _layer_norm(DN_ALPHA * x_ref[...] + mix, g_ref[...], b_ref[...])
    o_ref[...] = h
    _route(h, rwt_ref[...], rb_ref[...], comb_ref)


def _outproj_ln(ys, ws, x2d, g, b, rwt, rbias, tm=512):
    n, d = x2d.shape
    in_specs = [pl.BlockSpec((tm, y.shape[1]), lambda i: (i, 0)) for y in ys]
    in_specs += [pl.BlockSpec(w.shape, lambda i: (0, 0)) for w in ws]
    in_specs += [pl.BlockSpec((tm, d), lambda i: (i, 0)),
                 pl.BlockSpec((1, d), lambda i: (0, 0)), pl.BlockSpec((1, d), lambda i: (0, 0)),
                 pl.BlockSpec((N_EXPERTS, d), lambda i: (0, 0)),
                 pl.BlockSpec((N_EXPERTS, 1), lambda i: (0, 0))]
    return pl.pallas_call(
        _outproj_ln_kernel, grid=(n // tm,), in_specs=in_specs,
        out_specs=[pl.BlockSpec((tm, d), lambda i: (i, 0)),
                   pl.BlockSpec((N_EXPERTS, tm), lambda i: (0, i))],
        out_shape=[jax.ShapeDtypeStruct((n, d), F32), jax.ShapeDtypeStruct((N_EXPERTS, n), F32)],
        compiler_params=_cparams("parallel"), name="out_proj_ln",
    )(*ys, *ws, x2d, g.reshape(1, -1), b.reshape(1, -1), rwt, rbias.reshape(-1, 1))


def _route(h, rwt, rb, o_ref):
    logits = _dot3_nt(rwt, h)
    s = _sigmoid(logits)
    sel = s + rb
    s_rows = [s[e:e + 1, :] for e in range(N_EXPERTS)]
    rows = [sel[e:e + 1, :] for e in range(N_EXPERTS)]
    best_val, best = None, None
    for gidx in range(N_GROUPS):
        mem = rows[gidx * EXPERTS_PER_GROUP:(gidx + 1) * EXPERTS_PER_GROUP]
        gs = None
        for i in range(EXPERTS_PER_GROUP):
            for j in range(i + 1, EXPERTS_PER_GROUP):
                pair = mem[i] + mem[j]
                gs = pair if gs is None else jnp.maximum(gs, pair)
        if best_val is None:
            best_val, best = gs, jnp.zeros(gs.shape, I32)
        else:
            upd = gs > best_val
            best = jnp.where(upd, gidx, best)
            best_val = jnp.where(upd, gs, best_val)
    vals = [jnp.where(best == (e // EXPERTS_PER_GROUP), rows[e], -jnp.inf) for e in range(N_EXPERTS)]

    def arg_top(vs):
        m = functools.reduce(jnp.maximum, vs)
        idx = jnp.full(m.shape, N_EXPERTS, I32)
        for e in reversed(range(N_EXPERTS)):
            idx = jnp.where(vs[e] == m, e, idx)
        return idx

    i1 = arg_top(vals)
    i2 = arg_top([jnp.where(i1 == e, -jnp.inf, vals[e]) for e in range(N_EXPERTS)])
    g1 = functools.reduce(jnp.add, [jnp.where(i1 == e, s_rows[e], 0.0) for e in range(N_EXPERTS)])
    g2 = functools.reduce(jnp.add, [jnp.where(i2 == e, s_rows[e], 0.0) for e in range(N_EXPERTS)])
    tot = g1 + g2
    for e in range(N_EXPERTS):
        o_ref[e:e + 1, :] = jnp.where(i1 == e, g1 / tot, 0.0) + jnp.where(i2 == e, g2 / tot, 0.0)


MOE_EXPERTS_PER_STEP = 4


def _moe_kernel(h_ref, comb_ref, wgu_ref, wd_ref, g_ref, b_ref, o_ref, acc_ref, hb_ref):
    s = pl.program_id(1)

    @pl.when(s == 0)
    def _():
        acc_ref[...] = jnp.zeros_like(acc_ref)
        hb_ref[...] = h_ref[...].astype(BF16)

    hb = hb_ref[...]
    comb = comb_ref[...]
    lane = lax.broadcasted_iota(I32, comb.shape, 1)
    acts = []
    for j in range(MOE_EXPERTS_PER_STEP):
        e = s * MOE_EXPERTS_PER_STEP + j
        gu = jnp.dot(hb, wgu_ref[j], preferred_element_type=F32)
        gt, up = gu[:, :D_EXPERT], gu[:, D_EXPERT:]
        ce = jnp.sum(jnp.where(lane == e, comb, 0.0), axis=-1, keepdims=True)
        acts.append(((gt * _sigmoid(gt)) * up * ce).astype(BF16))
    wd = wd_ref[...].reshape(MOE_EXPERTS_PER_STEP * D_EXPERT, wd_ref.shape[2])
    acc_ref[...] += jnp.dot(jnp.concatenate(acts, axis=1), wd, preferred_element_type=F32)

    @pl.when(s == pl.num_programs(1) - 1)
    def _():
        o_ref[...] = _layer_norm(DN_ALPHA * h_ref[...] + acc_ref[...], g_ref[...], b_ref[...])


def _moe(h2d, comb, wgu, wd, g, b, tm=1024):
    n, d = h2d.shape
    tm = min(tm, n)
    eps = MOE_EXPERTS_PER_STEP
    return pl.pallas_call(
        _moe_kernel, grid=(n // tm, N_EXPERTS // eps),
        in_specs=[pl.BlockSpec((tm, d), lambda i, e: (i, 0)),
                  pl.BlockSpec((tm, N_EXPERTS), lambda i, e: (i, 0)),
                  pl.BlockSpec((eps, d, 2 * D_EXPERT), lambda i, e: (e, 0, 0)),
                  pl.BlockSpec((eps, D_EXPERT, d), lambda i, e: (e, 0, 0)),
                  pl.BlockSpec((1, d), lambda i, e: (0, 0)),
                  pl.BlockSpec((1, d), lambda i, e: (0, 0))],
        out_specs=pl.BlockSpec((tm, d), lambda i, e: (i, 0)),
        out_shape=jax.ShapeDtypeStruct((n, d), F32),
        scratch_shapes=[pltpu.VMEM((tm, d), F32), pltpu.VMEM((tm, d), BF16)],
        compiler_params=_cparams("parallel", "arbitrary"), name="moe",
    )(h2d, comb, wgu, wd, g.reshape(1, -1), b.reshape(1, -1))


def _rope_full(z, cos, sin_signed):
    return z * cos + pltpu.roll(z, LANE // 2, axis=1) * sin_signed


def _rope_half(z, cos, sin_signed, first_half):
    partner = jnp.where(first_half, pltpu.roll(z, LANE - C_IDX_DIM // 2, axis=1),
                        pltpu.roll(z, C_IDX_DIM // 2, axis=1))
    return z * cos + partner * sin_signed


def _dsa_prep_kernel(x_ref, w_ref, c128_ref, s128_ref, c64_ref, s64_ref, lng_ref, lnb_ref,
                     q_ref, k_ref, v_ref, iq_ref, ika_ref, ikb_ref, iw_ref):
    p = jnp.dot(x_ref[0].astype(BF16), w_ref[...], preferred_element_type=F32)
    c128, s128 = c128_ref[...], s128_ref[...]
    c64, s64 = c64_ref[...], s64_ref[...]
    lane = lax.broadcasted_iota(I32, c64.shape, 1)
    first_half = (lane & (C_IDX_DIM - 1)) < C_IDX_DIM // 2
    for h in range(C_HEADS):
        hs = slice(h * LANE, (h + 1) * LANE)
        q_ref[0, :, hs] = (_rope_full(p[:, hs], c128, s128) * Q_SCALE).astype(BF16)
    k0 = C_WIDTH
    k_ref[0] = _rope_full(p[:, k0:k0 + LANE], c128, s128).astype(BF16)
    v_ref[0] = p[:, k0 + LANE:k0 + 2 * LANE].astype(BF16)
    i0 = k0 + 2 * LANE
    for j in range(C_IDX_HEADS * C_IDX_DIM // LANE):
        js = slice(j * LANE, (j + 1) * LANE)
        z = p[:, i0 + j * LANE:i0 + (j + 1) * LANE]
        iq_ref[0, :, js] = (_rope_half(z, c64, s64, first_half) * C_IDX_DIM ** -0.5).astype(BF16)
    t0 = i0 + C_IDX_HEADS * C_IDX_DIM
    tile = p[:, t0:t0 + LANE]
    is_key = lane < C_IDX_DIM
    mu = jnp.sum(jnp.where(is_key, tile, 0.0), axis=-1, keepdims=True) * (1.0 / C_IDX_DIM)
    zc = jnp.where(is_key, tile - mu, 0.0)
    var = jnp.sum(zc * zc, axis=-1, keepdims=True) * (1.0 / C_IDX_DIM)
    ikn = zc * lax.rsqrt(var + LN_EPS) * lng_ref[...] + lnb_ref[...]
    ikr = jnp.where(is_key, _rope_half(ikn, c64, s64, first_half), 0.0)
    ika_ref[0] = ikr.astype(BF16)
    ikb_ref[0] = pltpu.roll(ikr, C_IDX_DIM, axis=1).astype(BF16)
    iw_ref[0] = tile * C_IDX_HEADS ** -0.5


def _dsa_prep(x3, w, tabs, lng, lnb, tm=512):
    b, t, d = x3.shape
    tm = min(tm, t)
    tab = pl.BlockSpec((tm, LANE), lambda i, j: (j, 0))
    par = pl.BlockSpec((1, LANE), lambda i, j: (0, 0))
    o = lambda wd: pl.BlockSpec((1, tm, wd), lambda i, j: (i, j, 0))
    widths = [C_WIDTH, LANE, LANE, C_IDX_HEADS * C_IDX_DIM, LANE, LANE, LANE]
    dtypes = [BF16, BF16, BF16, BF16, BF16, BF16, F32]
    return pl.pallas_call(
        _dsa_prep_kernel, grid=(b, t // tm),
        in_specs=[pl.BlockSpec((1, tm, d), lambda i, j: (i, j, 0)),
                  pl.BlockSpec(w.shape, lambda i, j: (0, 0)), tab, tab, tab, tab, par, par],
        out_specs=[o(wd) for wd in widths],
        out_shape=[jax.ShapeDtypeStruct((b, t, wd), dt) for wd, dt in zip(widths, dtypes)],
        compiler_params=_cparams("parallel", "parallel"), name="dsa_proj_prep",
    )(x3, w, *tabs, lng, lnb)


def _fold8(z, op):
    return op(z.reshape(z.shape[0] // 8, 8, z.shape[1]), axis=0)


def _dsa_attn_kernel(q_ref, iq_ref, iw_ref, k_ref, vt_ref, ikab_ref, o_ref,
                     sc_ref, tie_ref, bias_ref, s_ref, acc_ref, *, ktop, idx_bits):
    i = pl.program_id(1)
    tq = q_ref.shape[2]
    kg = k_ref.shape[2]
    assert tq == LANE and kg % tq == 0
    ng = i // (kg // tq) + 1
    keypos0 = lax.broadcasted_iota(I32, (kg, tq), 0)
    qpos = i * tq + lax.broadcasted_iota(I32, (kg, tq), 1)

    iqt = iq_ref[0]
    rhs_i = jnp.concatenate([iqt[:LANE], iqt[LANE:]], axis=1)
    w = [iw_ref[0, h:h + 1, :] for h in range(C_IDX_HEADS)]

    def score_body(g, carry):
        sab = jnp.dot(ikab_ref[0, g], rhs_i, preferred_element_type=F32)
        relu = lambda z: jnp.maximum(z, 0.0)
        sc = (relu(sab[:kg, :tq]) * w[0] + relu(sab[kg:, :tq]) * w[1]
              + relu(sab[:kg, tq:]) * w[2] + relu(sab[kg:, tq:]) * w[3]) + 0.0
        sc_ref[g] = jnp.where(g * kg + keypos0 <= qpos, sc, -jnp.inf)
        return carry

    lax.fori_loop(0, ng, score_body, 0)

    def count(ref, pred, n_groups=None):
        def body(g, acc):
            return acc + _fold8(jnp.where(pred(ref[g]), 1, 0), jnp.sum)
        acc = jnp.zeros((8, tq), I32)
        if n_groups is None:
            acc = lax.fori_loop(0, ng, body, acc)
        else:
            for g in range(n_groups):
                acc = body(g, acc)
        return jnp.sum(acc, axis=0, keepdims=True)

    def key_to_float(key):
        return pltpu.bitcast(key ^ ((key >> 31) & 0x7FFFFFFF), F32)

    def bisect_for(n_groups):
        def run():
            def bisect(it, carry):
                tkey, c_lo = carry
                ckey = tkey + lax.shift_left(jnp.int32(1), 31 - it)
                cand = key_to_float(ckey)
                cnt = jnp.where(ckey <= KEY_NEG_INF, n_groups * kg,
                                count(sc_ref, lambda x: x >= cand, n_groups))
                take = cnt >= ktop
                return jnp.where(take, ckey, tkey), jnp.where(take, cnt, c_lo)

            return lax.fori_loop(0, 32, bisect, (jnp.full((1, tq), INT_MIN, I32),
                                                 jnp.full((1, tq), n_groups * kg, I32)))
        return run

    tkey, c_lo = lax.switch(ng - 1, [bisect_for(n) for n in range(1, sc_ref.shape[0] + 1)])
    tau = key_to_float(tkey)

    tie_rows = jnp.where(c_lo > ktop, jnp.where(tau >= F32_LOWEST, 1, 0), 0)
    need_tie = jnp.max(tie_rows)

    @pl.when(need_tie == 0)
    def _():
        thr = jnp.maximum(tau, F32_LOWEST)

        def body(g, carry):
            bias_ref[g] = jnp.where(sc_ref[g] >= thr, 0.0, NEG_BIG)
            return carry

        lax.fori_loop(0, ng, body, 0)

    @pl.when(need_tie > 0)
    def _():
        need = ktop - count(sc_ref, lambda x: x > tau)

        def tie_body(g, carry):
            sc = sc_ref[g]
            idx = jnp.where(sc >= F32_LOWEST, g * kg + keypos0, TIE_NONE)
            tie_ref[g] = jnp.where(sc == tau, idx, TIE_NONE)
            return carry

        lax.fori_loop(0, ng, tie_body, 0)

        def bisect_idx(it, ans):
            cand = ans + lax.shift_left(jnp.int32(1), idx_bits - 1 - it)
            cnt = count(tie_ref, lambda x: x < cand)
            return jnp.where(cnt < need, cand, ans)

        jstar = lax.fori_loop(0, idx_bits, bisect_idx, jnp.zeros((1, tq), I32))
        thr = jnp.maximum(tau, F32_LOWEST)

        def body(g, carry):
            sel = jnp.where(sc_ref[g] > thr, 1, jnp.where(tie_ref[g] <= jstar, 1, 0))
            bias_ref[g] = jnp.where(sel > 0, 0.0, NEG_BIG)
            return carry

        lax.fori_loop(0, ng, body, 0)

    qt = q_ref[0]
    pairs = range(C_HEADS // 2)
    q_cols = [jnp.concatenate([qt[(2 * j) * LANE:(2 * j + 1) * LANE],
                               qt[(2 * j + 1) * LANE:(2 * j + 2) * LANE]], axis=1) for j in pairs]

    def pass1(g, m8s):
        k_g = k_ref[0, g]
        bias2 = jnp.concatenate([bias_ref[g]] * 2, axis=1)
        out = []
        for j in pairs:
            s2 = jnp.dot(k_g, q_cols[j], preferred_element_type=F32) + bias2
            s_ref[j, g] = s2
            out.append(jnp.maximum(m8s[j], _fold8(s2, jnp.max)))
        return tuple(out)

    m8s = lax.fori_loop(0, ng, pass1, tuple(jnp.full((8, 2 * tq), NEG_BIG, F32) for _ in pairs))
    ms = [jnp.max(m8, axis=0, keepdims=True) for m8 in m8s]
    acc_ref[...] = jnp.zeros_like(acc_ref)

    def pass2(g, l8s):
        vt_g = vt_ref[0, g]
        out = []
        for j in pairs:
            p2 = jnp.exp2(s_ref[j, g] - ms[j])
            acc_ref[j] += jnp.dot(vt_g, p2.astype(BF16), preferred_element_type=F32)
            out.append(l8s[j] + _fold8(p2, jnp.sum))
        return tuple(out)

    l8s = lax.fori_loop(0, ng, pass2, tuple(jnp.zeros((8, 2 * tq), F32) for _ in pairs))
    for j in pairs:
        out2 = acc_ref[j] / jnp.sum(l8s[j], axis=0, keepdims=True)
        for u in range(2):
            h = 2 * j + u
            o_ref[0, :, h * LANE:(h + 1) * LANE] = out2[:, u * tq:(u + 1) * tq].T.astype(o_ref.dtype)


def _dsa_attn(qt, iqt, iwt, kg4, vtg4, ikab4, ktop):
    b, _, t = qt.shape
    tq = LANE
    ngt, kg = kg4.shape[1], kg4.shape[2]
    qspec = lambda w: pl.BlockSpec((1, tq, w), lambda i, j: (i, j, 0))
    whole = lambda z: pl.BlockSpec((1,) + z.shape[1:], lambda i, j: (i, 0, 0, 0))
    kern = functools.partial(_dsa_attn_kernel, ktop=ktop, idx_bits=int(math.log2(t)))
    scratch = [pltpu.VMEM((ngt, kg, tq), F32), pltpu.VMEM((ngt, kg, tq), I32),
               pltpu.VMEM((ngt, kg, tq), F32), pltpu.VMEM((C_HEADS // 2, ngt, kg, 2 * tq), F32),
               pltpu.VMEM((C_HEADS // 2, C_HEAD_DIM, 2 * tq), F32)]
    return pl.pallas_call(
        kern, grid=(b, t // tq),
        in_specs=[pl.BlockSpec((1, C_WIDTH, tq), lambda i, j: (i, 0, j)),
                  pl.BlockSpec((1, C_IDX_HEADS * C_IDX_DIM, tq), lambda i, j: (i, 0, j)),
                  pl.BlockSpec((1, C_IDX_HEADS, tq), lambda i, j: (i, 0, j)),
                  whole(kg4), whole(vtg4), whole(ikab4)],
        out_specs=qspec(C_WIDTH),
        out_shape=jax.ShapeDtypeStruct((b, t, C_WIDTH), BF16),
        scratch_shapes=scratch,
        compiler_params=_cparams("parallel", "arbitrary"), name="dsa_attn",
    )(qt, iqt, iwt, kg4, vtg4, ikab4)


def _rope_tables(t):
    pos = jnp.arange(t, dtype=F32)

    def tab(dim):
        half = dim // 2
        inv = ROPE_THETA ** (-jnp.arange(half, dtype=F32) / half)
        ang = pos[:, None] * inv[None, :]
        cos, sin = jnp.cos(ang), jnp.sin(ang)
        reps = LANE // dim
        return (jnp.tile(jnp.concatenate([cos, cos], axis=-1), (1, reps)),
                jnp.tile(jnp.concatenate([-sin, sin], axis=-1), (1, reps)))

    c128, s128 = tab(C_HEAD_DIM)
    c64, s64 = tab(C_IDX_DIM)
    return c128, s128, c64, s64


def _pad_cols(w, cols):
    return jnp.pad(w, ((0, 0), (0, cols - w.shape[1])))


def _moe_block(h, comb_t, w_gate, w_up, w_down, g, b):
    wgu = jnp.concatenate([w_gate, w_up], axis=-1).astype(BF16)
    return _moe(h, comb_t.T, wgu, w_down.astype(BF16), g, b)


@jax.jit
def _forward(x, w_in_even, a_mu, a_w0, a_w2, a_a0, a_a2, a_g2, a_kk_scale, a_ka_scale, a_r_k,
             a_gn_g, a_gn_b, b_gate_w2, b_gate_b, b_norm_g, w_out_even, w_in_odd, c_ik_ln_g,
             c_ik_ln_b, w_out_odd, ln1_g, ln1_b, ln2_g, ln2_b, router_w, router_bias,
             exp_w_gate, exp_w_up, exp_w_down):
    bsz, t, d = x.shape
    assert d == D_MODEL and t % 256 == 0 and (t & (t - 1)) == 0
    n = bsz * t
    ktop = min(C_INDEX_TOPK, t // 4)
    rwt = router_w.T
    xf = x.reshape(n, d)
    hsum = (jnp.arange(A_WIDTH)[:, None] // A_HEAD_DIM
            == jnp.arange(A_WIDTH)[None, :] // A_HEAD_DIM).astype(BF16)

    for l in range(DEPTH):
        i = l // 2
        if l % 2 == 0:
            w = w_in_even[i]
            pa, pb = _proj(xf, [w[:, :A_COLS].astype(BF16),
                                _pad_cols(w[:, A_COLS:], B_PAD_COLS).astype(BF16)])
            wwa = jnp.zeros((LANE, 2 * A_WIDTH), F32)
            wwa = wwa.at[:A_DECAY_LORA, :A_WIDTH].set(a_w2[i]).at[A_DECAY_LORA:, A_WIDTH:].set(a_a2[i])
            ya = _rwkv(pa.reshape(bsz, t, A_COLS), a_mu[i], wwa, a_w0[i], a_a0[i], a_g2[i],
                       a_kk_scale[i], a_ka_scale[i], hsum, a_r_k[i].reshape(-1), a_gn_g[i], a_gn_b[i])
            gw2p = jnp.zeros((LANE, B_KEY_WIDTH), F32).at[:B_GATE_LORA].set(b_gate_w2[i])
            yb = _gla(pb.reshape(bsz, t, B_PAD_COLS), gw2p, b_gate_b[i], b_norm_g[i])
            wo = w_out_even[i].astype(BF16)
            h, comb_t = _outproj_ln([ya.reshape(n, A_WIDTH), yb.reshape(n, B_VAL_WIDTH)],
                                    [wo[:A_WIDTH], wo[A_WIDTH:]], xf, ln1_g[l], ln1_b[l],
                                    rwt, router_bias)
        else:
            pad = lambda z: jnp.pad(z, (0, LANE - z.shape[0])).reshape(1, LANE)
            q, k, v, iq, ika, ikb, iw = _dsa_prep(
                xf.reshape(bsz, t, d), _pad_cols(w_in_odd[i], ODD_PAD_COLS).astype(BF16),
                _rope_tables(t), pad(c_ik_ln_g[i]), pad(c_ik_ln_b[i]))
            kgrp = min(KEY_GROUP, t)
            grp = lambda z: z.reshape(bsz, t // kgrp, kgrp, LANE)
            iwt = jnp.swapaxes(iw[:, :, C_IDX_DIM:C_IDX_DIM + C_IDX_HEADS], 1, 2)
            att = _dsa_attn(jnp.swapaxes(q, 1, 2), jnp.swapaxes(iq, 1, 2), iwt, grp(k),
                            jnp.swapaxes(grp(v), 2, 3),
                            jnp.concatenate([grp(ika), grp(ikb)], axis=2), ktop)
            h, comb_t = _outproj_ln([att.reshape(n, C_WIDTH)], [w_out_odd[i].astype(BF16)], xf,
                                    ln1_g[l], ln1_b[l], rwt, router_bias)
        xf = _moe_block(h, comb_t, exp_w_gate[l], exp_w_up[l], exp_w_down[l], ln2_g[l], ln2_b[l])
    return xf.reshape(bsz, t, d)


def kernel(x, w_in_even, a_mu, a_w0, a_w2, a_a0, a_a2, a_g2, a_kk_scale, a_ka_scale, a_r_k, a_gn_g, a_gn_b, b_gate_w2, b_gate_b, b_norm_g, w_out_even, w_in_odd, c_ik_ln_g, c_ik_ln_b, w_out_odd, ln1_g, ln1_b, ln2_g, ln2_b, router_w, router_bias, exp_w_gate, exp_w_up, exp_w_down):
    return _forward(x, w_in_even, a_mu, a_w0, a_w2, a_a0, a_a2, a_g2, a_kk_scale, a_ka_scale, a_r_k,
                    a_gn_g, a_gn_b, b_gate_w2, b_gate_b, b_norm_g, w_out_even, w_in_odd, c_ik_ln_g,
                    c_ik_ln_b, w_out_odd, ln1_g, ln1_b, ln2_g, ln2_b, router_w, router_bias,
                    exp_w_gate, exp_w_up, exp_w_down)
```
